```python
import jax, jax.numpy as jnp
from jax import lax
import numpy as np

D_MODEL = 1024
BATCH = 8
SEQ = 2048
DEPTH = 2

MEM_LEN = 256
POOL_WIDTH = 512
POOL_GROUPS = 4
POOL_GROUP_DIM = POOL_WIDTH // POOL_GROUPS
POOL_WINDOWS = (2, 4, 8, 16)
MEM_HEADS = 4
MEM_HEAD_DIM = 128
MEM_WIDTH = MEM_HEADS * MEM_HEAD_DIM
RWKV_HEAD_DIM = 64
RWKV_WIDTH = D_MODEL
RWKV_HEADS = RWKV_WIDTH // RWKV_HEAD_DIM
DECAY_LORA = 64
ICLR_LORA = 64
VRES_LORA = 32
GATE_LORA = 128
RWKV_COLS = 3 * RWKV_WIDTH + DECAY_LORA + ICLR_LORA + GATE_LORA
N_BRANCH = 3
OFF_Q = POOL_WIDTH
OFF_RWKV = OFF_Q + MEM_WIDTH
OFF_GATE = OFF_RWKV + RWKV_COLS
IN_COLS = OFF_GATE + N_BRANCH * D_MODEL
D_FF = 2816
CONV_WIDTH = 3
NORM_EPS = 1e-6
LNX_EPS = 64e-5
L2_EPS = 1e-12

kernel_name = "hybrid_pool_rwkv7_memxattn_convffn"


def rms_norm(x, g):
    xf = x.astype(jnp.float32)
    y = xf * lax.rsqrt(jnp.mean(xf * xf, axis=-1, keepdims=True) + NORM_EPS)
    return (y * g.astype(jnp.float32)).astype(x.dtype)


def token_shift(z):
    return jnp.pad(z, ((0, 0), (1, 0), (0, 0)))[:, :-1]


def pool_branch(zp, pool_w, pool_b, pool_scale):
    B, S, _ = zp.shape
    p = zp.astype(jnp.float32).reshape(B, S, POOL_GROUPS, POOL_GROUP_DIM)
    cs = jnp.cumsum(p, axis=1)
    t = jnp.arange(S)
    outs = []
    for gi, win in enumerate(POOL_WINDOWS):
        c = cs[:, :, gi]
        c_lag = jnp.pad(c, ((0, 0), (win, 0), (0, 0)))[:, :S]
        cnt = jnp.minimum(t + 1, win).astype(jnp.float32)[None, :, None]
        outs.append((c - c_lag) / cnt - p[:, :, gi])
    pooled = jnp.stack(outs, axis=2).astype(zp.dtype)
    mixed = jnp.einsum('bsgc,gcd->bsgd', pooled, pool_w).reshape(B, S, POOL_WIDTH) + pool_b
    return mixed * pool_scale


def mem_attention(zq, mem_n, w_mem_kv):
    B, S, _ = zq.shape
    M = mem_n.shape[1]
    q = zq.reshape(B, S, MEM_HEADS, MEM_HEAD_DIM)
    kv = mem_n @ w_mem_kv
    k, v = jnp.split(kv, 2, axis=-1)
    k = k.reshape(B, M, MEM_HEADS, MEM_HEAD_DIM)
    v = v.reshape(B, M, MEM_HEADS, MEM_HEAD_DIM)
    s = jnp.einsum('bshd,bmhd->bhsm', q, k).astype(jnp.float32) * (MEM_HEAD_DIM ** -0.5)
    pr = jax.nn.softmax(s, axis=-1).astype(zq.dtype)
    o = jnp.einsum('bhsm,bmhd->bshd', pr, v)
    return o.reshape(B, S, MEM_WIDTH)


def rwkv7_scan(r, w, k, v, kk_neg, kk_a):
    B, _, H, N = r.shape

    def step(state, inp):
        rt, wt, kt, vt, an, bn = inp
        sa = jnp.einsum('bhvk,bhk->bhv', state, an)
        state = state * wt[:, :, None, :] + sa[..., None] * bn[:, :, None, :] + vt[..., None] * kt[:, :, None, :]
        yt = jnp.einsum('bhvk,bhk->bhv', state, rt)
        return state, yt

    xs = tuple(jnp.moveaxis(a, 1, 0) for a in (r, w, k, v, kk_neg, kk_a))
    s0 = jnp.zeros((B, H, N, N), jnp.float32)
    _, ys = lax.scan(step, s0, xs)
    return jnp.moveaxis(ys, 0, 1)


def rwkv_branch(r, k, v, dw, da, dg, w0, w_up_decay, a0, w_up_a, w_up_g, k_k, k_a, r_k, ln_x_w, ln_x_b):
    B, S, _ = r.shape
    dt = r.dtype
    f32 = jnp.float32
    w_log = -jax.nn.softplus(-(w0 + jnp.tanh(dw) @ w_up_decay).astype(f32)) - 0.5
    decay = jnp.exp(-jnp.exp(w_log))
    a = jax.nn.sigmoid((a0 + da @ w_up_a).astype(f32))
    g = jax.nn.sigmoid(dg) @ w_up_g
    hs = (B, S, RWKV_HEADS, RWKV_HEAD_DIM)
    kk = (k * k_k).astype(f32).reshape(hs)
    kk = kk * lax.rsqrt(jnp.sum(kk * kk, axis=-1, keepdims=True) + L2_EPS)
    k_eff = k.astype(f32) * (1.0 + (a - 1.0) * k_a.astype(f32))
    rh = r.astype(f32).reshape(hs)
    kh = k_eff.reshape(hs)
    vh = v.astype(f32).reshape(hs)
    wh = decay.reshape(hs)
    ah = a.reshape(hs)
    y = rwkv7_scan(rh, wh, kh, vh, -kk, kk * ah)
    mu = jnp.mean(y, axis=-1, keepdims=True)
    var = jnp.mean(jnp.square(y - mu), axis=-1, keepdims=True)
    y = ((y - mu) * lax.rsqrt(var + LNX_EPS)).reshape(B, S, RWKV_WIDTH)
    y = y * ln_x_w.astype(f32) + ln_x_b.astype(f32)
    bonus = jnp.sum(rh * kh * r_k.astype(f32), axis=-1, keepdims=True) * vh
    y = y + bonus.reshape(B, S, RWKV_WIDTH)
    return y.astype(dt) * g


def causal_dwconv(u, w, b):
    S = u.shape[1]
    up = jnp.pad(u, ((0, 0), (CONV_WIDTH - 1, 0), (0, 0)))
    out = b
    for j in range(CONV_WIDTH):
        out = out + w[j] * up[:, j:j + S]
    return out


def setup_inputs(seed: int = 0) -> dict:
    key = jax.random.key(seed)
    ks = iter(jax.random.split(key, 40))
    f32 = jnp.float32

    def nrm(shape, scale):
        return jax.random.normal(next(ks), shape, f32) * scale

    def gain(shape):
        return 1.0 + nrm(shape, 0.05)

    L, D = DEPTH, D_MODEL
    return {
        "x": nrm((BATCH, SEQ, D), 1.0),
        "mem": nrm((BATCH, MEM_LEN, D), 1.0),
        "mem_norm": gain((D,)),
        "norm_mix_pre": gain((L, D)),
        "norm_mix_post": gain((L, D)),
        "w_in": nrm((L, D, IN_COLS), D ** -0.5),
        "mu_shift": jax.random.uniform(next(ks), (L, RWKV_COLS), f32, 0.1, 0.9),
        "pool_w": nrm((L, POOL_GROUPS, POOL_GROUP_DIM, POOL_GROUP_DIM), POOL_GROUP_DIM ** -0.5),
        "pool_b": nrm((L, POOL_WIDTH), 0.01),
        "pool_scale": 1.0 + nrm((L, POOL_WIDTH), 0.1),
        "w_proj_pool": nrm((L, POOL_WIDTH, D), POOL_WIDTH ** -0.5),
        "w_mem_kv": nrm((L, D, 2 * MEM_WIDTH), D ** -0.5),
        "w_proj_mem": nrm((L, MEM_WIDTH, D), MEM_WIDTH ** -0.5),
        "w0": jax.random.uniform(next(ks), (L, RWKV_WIDTH), f32, -5.0, 1.0),
        "w_up_decay": nrm((L, DECAY_LORA, RWKV_WIDTH), DECAY_LORA ** -0.5),
        "a0": nrm((L, RWKV_WIDTH), 0.1),
        "w_up_a": nrm((L, ICLR_LORA, RWKV_WIDTH), ICLR_LORA ** -0.5),
        "w_up_g": nrm((L, GATE_LORA, RWKV_WIDTH), GATE_LORA ** -0.5),
        "k_k": 0.85 + nrm((L, RWKV_WIDTH), 0.05),
        "k_a": 1.0 + nrm((L, RWKV_WIDTH), 0.05),
        "r_k": nrm((L, RWKV_HEADS, RWKV_HEAD_DIM), 0.1),
        "ln_x_w": gain((L, RWKV_WIDTH)),
        "ln_x_b": nrm((L, RWKV_WIDTH), 0.01),
        "v0": nrm((L - 1, RWKV_WIDTH), 0.1),
        "w_down_v": nrm((L - 1, RWKV_WIDTH, VRES_LORA), RWKV_WIDTH ** -0.5),
        "w_up_v": nrm((L - 1, VRES_LORA, RWKV_WIDTH), VRES_LORA ** -0.5),
        "w_proj_rwkv": nrm((L, RWKV_WIDTH, D), RWKV_WIDTH ** -0.5),
        "gate_b": nrm((L, N_BRANCH, D), 0.01),
        "w_o": nrm((L, D, D), D ** -0.5),
        "norm_ffn_pre": gain((L, D)),
        "norm_ffn_post": gain((L, D)),
        "w_ffn_up": nrm((L, D, 2 * D_FF), D ** -0.5),
        "conv_w": nrm((L, CONV_WIDTH, 2 * D_FF), CONV_WIDTH ** -0.5),
        "conv_b": nrm((L, 2 * D_FF), 0.01),
        "w_ffn_down": nrm((L, D_FF, D), D_FF ** -0.5),
    }


def reference(x, mem, mem_norm, norm_mix_pre, norm_mix_post, w_in, mu_shift, pool_w, pool_b,
              pool_scale, w_proj_pool, w_mem_kv, w_proj_mem, w0, w_up_decay, a0, w_up_a, w_up_g,
              k_k, k_a, r_k, ln_x_w, ln_x_b, v0, w_down_v, w_up_v, w_proj_rwkv, gate_b, w_o,
              norm_ffn_pre, norm_ffn_post, w_ffn_up, conv_w, conv_b, w_ffn_down):
    B, S, D = x.shape
    mem_n = rms_norm(mem, mem_norm)
    v_first = None
    for l in range(DEPTH):
        h = rms_norm(x, norm_mix_pre[l])
        z = h @ w_in[l]
        z_pool, z_q, z_rwkv, z_gate = jnp.split(z, [OFF_Q, OFF_RWKV, OFF_GATE], axis=-1)

        y_pool = pool_branch(z_pool, pool_w[l], pool_b[l], pool_scale[l]) @ w_proj_pool[l]

        y_mem = mem_attention(z_q, mem_n, w_mem_kv[l]) @ w_proj_mem[l]

        zr = z_rwkv + (token_shift(z_rwkv) - z_rwkv) * mu_shift[l]
        r, k, v, dw, da, dg = jnp.split(
            zr, [RWKV_WIDTH, 2 * RWKV_WIDTH, 3 * RWKV_WIDTH, 3 * RWKV_WIDTH + DECAY_LORA,
                 3 * RWKV_WIDTH + DECAY_LORA + ICLR_LORA], axis=-1)
        if l == 0:
            v_first = v
        else:
            vg = jax.nn.sigmoid(v0[l - 1] + (v @ w_down_v[l - 1]) @ w_up_v[l - 1])
            v = v + (v_first - v) * vg
        y_rwkv = rwkv_branch(r, k, v, dw, da, dg, w0[l], w_up_decay[l], a0[l], w_up_a[l],
                             w_up_g[l], k_k[l], k_a[l], r_k[l], ln_x_w[l], ln_x_b[l]) @ w_proj_rwkv[l]

        gates = jax.nn.sigmoid(z_gate.reshape(B, S, N_BRANCH, D) + gate_b[l])
        merged = gates[:, :, 0] * y_pool + gates[:, :, 1] * y_rwkv + gates[:, :, 2] * y_mem
        x = x + rms_norm(merged @ w_o[l], norm_mix_post[l])

        h = rms_norm(x, norm_ffn_pre[l])
        u = causal_dwconv(h @ w_ffn_up[l], conv_w[l], conv_b[l])
        u_gate, u_val = jnp.split(u, 2, axis=-1)
        f = (jax.nn.gelu(u_gate, approximate=True) * u_val) @ w_ffn_down[l]
        x = x + rms_norm(f, norm_ffn_post[l])
    return x
```

```python
import functools
import math

import jax
import jax.numpy as jnp
from jax import lax
from jax.experimental import pallas as pl
from jax.experimental.pallas import tpu as pltpu

F32 = jnp.float32
BF16 = jnp.bfloat16

D_MODEL = 1024
MEM_LEN = 256
POOL_WIDTH = 512
POOL_GROUP_DIM = 128
POOL_WINDOWS = (2, 4, 8, 16)
POOL_HALO = 16
MEM_HEADS = 4
MEM_HEAD_DIM = 128
MEM_WIDTH = MEM_HEADS * MEM_HEAD_DIM
RWKV_HEAD_DIM = 64
RWKV_WIDTH = D_MODEL
DECAY_LORA = 64
ICLR_LORA = 64
VRES_LORA = 32
GATE_LORA = 128
LORA_COLS = DECAY_LORA + ICLR_LORA + GATE_LORA
RWKV_COLS = 3 * RWKV_WIDTH + LORA_COLS
N_BRANCH = 3
OFF_Q = POOL_WIDTH
OFF_RWKV = OFF_Q + MEM_WIDTH
OFF_GATE = OFF_RWKV + RWKV_COLS
IN_COLS = OFF_GATE + N_BRANCH * D_MODEL
D_FF = 2816
CONV_WIDTH = 3
NORM_EPS = 1e-6
LNX_EPS = 64e-5
L2_EPS = 1e-12

LANES = 128
SUBLANES = 8
VMEM_LIMIT_BYTES = 56 * 1024 * 1024

SCAN_CHUNK = 64
HEADS_PER_SLAB = LANES // RWKV_HEAD_DIM
N_SLABS = RWKV_WIDTH // LANES
FFN_COLS = 256

ZBLK_PQ, ZBLK_G0, ZBLK_G1, ZBLK_G2, ZBLK_R, ZBLK_K, ZBLK_V = range(7)
ZBLK_LORA = 7 * D_MODEL // LORA_COLS


def _params(*sem):
    return pltpu.CompilerParams(dimension_semantics=sem, vmem_limit_bytes=VMEM_LIMIT_BYTES)


def _dot(a, b):
    return jnp.dot(a, b, preferred_element_type=F32)


def _dot_nt(a, b):
    return lax.dot_general(a, b, (((1,), (1,)), ((), ())), preferred_element_type=F32)


def _dot_tn(a, b):
    return lax.dot_general(a, b, (((0,), (0,)), ((), ())), preferred_element_type=F32)


def _sigmoid(x):
    return 1.0 / (1.0 + jnp.exp(-x))


def _rms(x, g):
    ms = jnp.mean(x * x, axis=-1, keepdims=True)
    return x * lax.rsqrt(ms + NORM_EPS) * g


def _split2(x):
    hi = x.astype(BF16)
    lo = (x - hi.astype(F32)).astype(BF16)
    return hi, lo


def _head_sum(x, seg):
    hi, lo = _split2(x)
    return _dot(hi, seg) + _dot(lo, seg)


def _seg_matrix():
    i = lax.broadcasted_iota(jnp.int32, (LANES, LANES), 0) // RWKV_HEAD_DIM
    j = lax.broadcasted_iota(jnp.int32, (LANES, LANES), 1) // RWKV_HEAD_DIM
    return jnp.where(i == j, 1.0, 0.0).astype(BF16)


def _norm_matmul_kernel(x_ref, g_ref, w_ref, o_ref, xn_ref):
    @pl.when(pl.program_id(1) == 0)
    def _():
        xn_ref[...] = _rms(x_ref[...], g_ref[...]).astype(BF16)

    o_ref[...] = _dot(xn_ref[...], w_ref[...])


def _norm_matmul(x, g, w, tm, tn):
    t, d = x.shape
    n = w.shape[1]
    return pl.pallas_call(
        _norm_matmul_kernel,
        grid=(t // tm, n // tn),
        in_specs=[pl.BlockSpec((tm, d), lambda i, j: (i, 0)),
                  pl.BlockSpec((1, d), lambda i, j: (0, 0)),
                  pl.BlockSpec((d, tn), lambda i, j: (0, j))],
        out_specs=pl.BlockSpec((tm, tn), lambda i, j: (i, j)),
        out_shape=jax.ShapeDtypeStruct((t, n), F32),
        scratch_shapes=[pltpu.VMEM((tm, d), BF16)],
        compiler_params=_params("parallel", "arbitrary"),
        name="norm_matmul",
    )(x, g, w)


def _branches_kernel(zpq_ref, halo_ref, zg0_ref, zg2_ref, kv_ref, poolw_ref, poolb_ref,
                     pools_ref, wpp_ref, wpm_ref, gb_ref, o_ref, *, tm):
    i = pl.program_id(1)
    p = zpq_ref[:, :POOL_WIDTH]
    prev = jnp.where(i == 0, 0.0, halo_ref[...])
    pe = jnp.concatenate([prev, p], axis=0)
    pos = lax.broadcasted_iota(jnp.int32, (tm, POOL_GROUP_DIM), 0) + i * tm
    mixed = []
    for gi, win in enumerate(POOL_WINDOWS):
        sl = slice(gi * POOL_GROUP_DIM, (gi + 1) * POOL_GROUP_DIM)
        s = pe[:, sl]
        k = 1
        while k < win:
            s = s + pltpu.roll(s, k, axis=0)
            k *= 2
        cnt = jnp.minimum(pos + 1, win).astype(F32)
        pooled = s[POOL_HALO:] / cnt - p[:, sl]
        m = _dot(pooled.astype(BF16), poolw_ref[gi]) + poolb_ref[:, sl]
        mixed.append((m * pools_ref[:, sl]).astype(BF16))
    y_pool = _dot(jnp.concatenate(mixed, axis=1), wpp_ref[...])

    heads = []
    for h in range(MEM_HEADS):
        q = zpq_ref[:, POOL_WIDTH + h * MEM_HEAD_DIM:POOL_WIDTH + (h + 1) * MEM_HEAD_DIM]
        kh = kv_ref[:, h * MEM_HEAD_DIM:(h + 1) * MEM_HEAD_DIM]
        vh = kv_ref[:, MEM_WIDTH + h * MEM_HEAD_DIM:MEM_WIDTH + (h + 1) * MEM_HEAD_DIM]
        s = _dot_nt(q.astype(BF16), kh.astype(BF16)) * (MEM_HEAD_DIM ** -0.5)
        s = s - jnp.max(s, axis=-1, keepdims=True)
        e = jnp.exp(s)
        pr = e / jnp.sum(e, axis=-1, keepdims=True)
        heads.append(_dot(pr.astype(BF16), vh.astype(BF16)).astype(BF16))
    y_mem = _dot(jnp.concatenate(heads, axis=1), wpm_ref[...])

    g0 = _sigmoid(zg0_ref[...] + gb_ref[0:1, :])
    g2 = _sigmoid(zg2_ref[...] + gb_ref[2:3, :])
    o_ref[...] = g0 * y_pool + g2 * y_mem


def _branches(z, kv, pool_w, pool_b, pool_scale, w_proj_pool, w_proj_mem, gate_b, batch, seq, tm):
    t = z.shape[0]
    nt = seq // tm
    hb = tm // POOL_HALO
    d = D_MODEL
    const = lambda *shape: pl.BlockSpec(shape, lambda b, i: (0,) * len(shape))
    return pl.pallas_call(
        functools.partial(_branches_kernel, tm=tm),
        grid=(batch, nt),
        in_specs=[pl.BlockSpec((tm, d), lambda b, i: (b * nt + i, ZBLK_PQ)),
                  pl.BlockSpec((POOL_HALO, POOL_WIDTH),
                               lambda b, i: (jnp.maximum((b * nt + i) * hb - 1, 0), 0)),
                  pl.BlockSpec((tm, d), lambda b, i: (b * nt + i, ZBLK_G0)),
                  pl.BlockSpec((tm, d), lambda b, i: (b * nt + i, ZBLK_G2)),
                  pl.BlockSpec((MEM_LEN, 2 * MEM_WIDTH), lambda b, i: (b, 0)),
                  const(len(POOL_WINDOWS), POOL_GROUP_DIM, POOL_GROUP_DIM),
                  const(1, POOL_WIDTH), const(1, POOL_WIDTH),
                  const(POOL_WIDTH, d), const(MEM_WIDTH, d), const(N_BRANCH, d)],
        out_specs=pl.BlockSpec((tm, d), lambda b, i: (b * nt + i, 0)),
        out_shape=jax.ShapeDtypeStruct((t, d), F32),
        compiler_params=_params("parallel", "arbitrary"),
        name="branches",
    )(z, z, z, z, kv, pool_w, pool_b, pool_scale, w_proj_pool, w_proj_mem, gate_b)


def _prep_kernel(*refs, tm, has_vres):
    (zr_ref, zk_ref, zv_ref, zl_ref, hr_ref, hk_ref, hv_ref, hl_ref,
     mu_ref, mul_ref, w0_ref, wdec_ref, a0_ref, wa_ref, wg_ref, kk_ref, ka_ref) = refs[:17]
    refs = refs[17:]
    if has_vres:
        vfirst_ref, v0_ref, wdv_ref, wuv_ref = refs[:4]
        refs = refs[4:]
    r_o, lw_o, k_o, v_o, kk_o, b_o, g_o = refs

    first = pl.program_id(1) == 0

    def shifted(z_ref, h_ref, mu):
        z = z_ref[...]
        row = lax.broadcasted_iota(jnp.int32, z.shape, 0)
        last = jnp.where(first, 0.0, h_ref[SUBLANES - 1:SUBLANES, :])
        zp = jnp.where(row == 0, last, pltpu.roll(z, 1, axis=0))
        return z + (zp - z) * mu

    r = shifted(zr_ref, hr_ref, mu_ref[0:1, :])
    k = shifted(zk_ref, hk_ref, mu_ref[1:2, :])
    v = shifted(zv_ref, hv_ref, mu_ref[2:3, :])
    lo = shifted(zl_ref, hl_ref, mul_ref[...])
    dwa = lo[:, :LANES]
    dg = lo[:, LANES:]

    if has_vres:
        low = _dot(v.astype(BF16), wdv_ref[...])
        vg = _sigmoid(v0_ref[...] + _dot(low.astype(BF16), wuv_ref[...]))
        v = v + (vfirst_ref[...] - v) * vg

    u = w0_ref[...] + _dot(jnp.tanh(dwa).astype(BF16), wdec_ref[...])
    lw_o[...] = -math.exp(-0.5) * _sigmoid(u)
    alpha = _sigmoid(a0_ref[...] + _dot(dwa.astype(BF16), wa_ref[...]))
    g_o[...] = _dot(_sigmoid(dg).astype(BF16), wg_ref[...])

    seg = _seg_matrix()
    kk = k * kk_ref[...]
    kk2 = kk * kk
    ss = jnp.concatenate([_head_sum(kk2[:, s * LANES:(s + 1) * LANES], seg) for s in range(N_SLABS)], axis=1)
    kk = kk * lax.rsqrt(ss + L2_EPS)
    r_o[...] = r
    k_o[...] = k * (1.0 + (alpha - 1.0) * ka_ref[...])
    v_o[...] = v
    kk_o[...] = kk
    b_o[...] = kk * alpha


def _prep(z, mu3, mul, w0, wdec, a0, wa, wg, k_k, k_a, vres, batch, seq, tm):
    t = z.shape[0]
    nt = seq // tm
    hb = tm // SUBLANES
    d = D_MODEL
    const = lambda *shape: pl.BlockSpec(shape, lambda b, i: (0,) * len(shape))
    main = lambda w, blk: pl.BlockSpec((tm, w), lambda b, i: (b * nt + i, blk))
    halo = lambda w, blk: pl.BlockSpec(
        (SUBLANES, w), lambda b, i: (jnp.maximum((b * nt + i) * hb - 1, 0), blk))
    in_specs = [main(d, ZBLK_R), main(d, ZBLK_K), main(d, ZBLK_V), main(LORA_COLS, ZBLK_LORA),
                halo(d, ZBLK_R), halo(d, ZBLK_K), halo(d, ZBLK_V), halo(LORA_COLS, ZBLK_LORA),
                const(3, d), const(1, LORA_COLS), const(1, d), const(LANES, d), const(1, d),
                const(LANES, d), const(GATE_LORA, d), const(1, d), const(1, d)]
    args = [z] * 8 + [mu3, mul, w0, wdec, a0, wa, wg, k_k, k_a]
    if vres is not None:
        v_first, v0, wdv, wuv = vres
        in_specs += [pl.BlockSpec((tm, d), lambda b, i: (b * nt + i, 0)),
                     const(1, d), const(d, LANES), const(LANES, d)]
        args += [v_first, v0, wdv, wuv]
    out_spec = pl.BlockSpec((tm, d), lambda b, i: (b * nt + i, 0))
    return pl.pallas_call(
        functools.partial(_prep_kernel, tm=tm, has_vres=vres is not None),
        grid=(batch, nt),
        in_specs=in_specs,
        out_specs=[out_spec] * 7,
        out_shape=[jax.ShapeDtypeStruct((t, d), F32)] * 7,
        compiler_params=_params("parallel", "arbitrary"),
        name="rwkv_prep",
    )(*args)


def _stack_heads(x, m1):
    return jnp.concatenate([jnp.where(m1, x, 0.0), jnp.where(m1, 0.0, x)], axis=0)


def _scan_kernel(r_ref, lw_ref, k_ref, v_ref, kk_ref, b_ref, g_ref, rk_ref, lnw_ref, lnb_ref,
                 o_ref, h_ref):
    c = SCAN_CHUNK
    c2 = HEADS_PER_SLAB * c

    @pl.when(pl.program_id(1) == 0)
    def _():
        h_ref[...] = jnp.zeros_like(h_ref)

    ti = lax.broadcasted_iota(jnp.int32, (c, c), 0)
    tj = lax.broadcasted_iota(jnp.int32, (c, c), 1)
    ltri = jnp.where(ti >= tj, 1.0, 0.0).astype(BF16)
    lw = lw_ref[...]
    hi = lw.astype(BF16)
    r1 = lw - hi.astype(F32)
    mid = r1.astype(BF16)
    lo = (r1 - mid.astype(F32)).astype(BF16)
    cum = _dot(ltri, hi) + _dot(ltri, mid) + _dot(ltri, lo)

    m1 = lax.broadcasted_iota(jnp.int32, (c, LANES), 1) < RWKV_HEAD_DIM
    si = lax.broadcasted_iota(jnp.int32, (c2, c2), 0)
    sj = lax.broadcasted_iota(jnp.int32, (c2, c2), 1)
    strict = si > sj
    incl = si >= sj
    eye = jnp.where(si == sj, 1.0, 0.0)
    seg = _seg_matrix()
    inv_n = 1.0 / RWKV_HEAD_DIM

    for s in range(N_SLABS):
        sl = slice(s * LANES, (s + 1) * LANES)
        cum_s = cum[:, sl]
        lw_s = lw[:, sl]
        last = cum_s[c - 1:c, :]
        p_in = jnp.exp(cum_s)
        p_ex = jnp.exp(cum_s - lw_s)
        p_inv = jnp.exp(-cum_s)
        p_end = jnp.exp(last - cum_s)
        rr = r_ref[:, sl]
        kx = k_ref[:, sl]
        vv = v_ref[:, sl]
        bb = b_ref[:, sl]
        a_t = -kk_ref[:, sl] * p_ex
        r_t = rr * p_in

        xl = jnp.concatenate([_stack_heads(a_t, m1), _stack_heads(r_t, m1)], axis=0).astype(BF16)
        xr = jnp.concatenate([_stack_heads(bb * p_inv, m1), _stack_heads(kx * p_inv, m1)],
                             axis=0).astype(BF16)
        gm = _dot_nt(xl, xr)
        d_ab = jnp.where(strict, gm[:c2, :c2], 0.0)
        d_ak = jnp.where(strict, gm[:c2, c2:], 0.0)
        d_rb = jnp.where(incl, gm[c2:, :c2], 0.0)
        d_rk = jnp.where(incl, gm[c2:, c2:], 0.0)

        tinv = eye + d_ab
        apow = d_ab.astype(BF16)
        n = 2
        while n < c:
            apow = _dot(apow, apow).astype(BF16)
            tinv = tinv + _dot(tinv.astype(BF16), apow)
            n *= 2

        h = h_ref[s]
        xh = _dot_nt(jnp.concatenate([a_t, r_t], axis=0).astype(BF16), h.astype(BF16))
        v_st = _stack_heads(vv, m1)
        z = _stack_heads(xh[:c], m1) + _dot(d_ak.astype(BF16), v_st.astype(BF16))
        u = _dot(tinv.astype(BF16), z.astype(BF16))
        uv = jnp.concatenate([u, v_st], axis=0).astype(BF16)
        y_st = _dot(jnp.concatenate([d_rb, d_rk], axis=1).astype(BF16), uv)
        y = y_st[:c] + y_st[c:] + xh[c:]

        bk = jnp.concatenate([_stack_heads(bb * p_end, m1), _stack_heads(kx * p_end, m1)],
                             axis=0).astype(BF16)
        h_ref[s] = h * jnp.exp(last) + _dot_tn(uv, bk)

        mu = _head_sum(y, seg) * inv_n
        dlt = y - mu
        var = _head_sum(dlt * dlt, seg) * inv_n
        yn = dlt * lax.rsqrt(var + LNX_EPS) * lnw_ref[:, sl] + lnb_ref[:, sl]
        bonus = _head_sum(rr * kx * rk_ref[:, sl], seg) * vv
        o_ref[:, sl] = ((yn + bonus) * g_ref[:, sl]).astype(BF16)


def _scan(r, lw, k, v, kk, b, g, r_k, ln_w, ln_b, batch, seq):
    t, d = r.shape
    c = SCAN_CHUNK
    nc = seq // c
    blk = pl.BlockSpec((c, d), lambda bi, ci: (bi * nc + ci, 0))
    row = pl.BlockSpec((1, d), lambda bi, ci: (0, 0))
    return pl.pallas_call(
        _scan_kernel,
        grid=(batch, nc),
        in_specs=[blk] * 7 + [row] * 3,
        out_specs=blk,
        out_shape=jax.ShapeDtypeStruct((t, d), BF16),
        scratch_shapes=[pltpu.VMEM((N_SLABS, LANES, LANES), F32)],
        compiler_params=_params("parallel", "arbitrary"),
        name="rwkv_scan",
    )(r, lw, k, v, kk, b, g, r_k, ln_w, ln_b)


def _merge_kernel(yg_ref, mp_ref, zg1_ref, x_ref, wpr_ref, wo_ref, gb_ref, gn_ref, o_ref):
    y_rwkv = _dot(yg_ref[...], wpr_ref[...])
    merged = mp_ref[...] + _sigmoid(zg1_ref[...] + gb_ref[1:2, :]) * y_rwkv
    o = _dot(merged.astype(BF16), wo_ref[...])
    o_ref[...] = x_ref[...] + _rms(o, gn_ref[...])


def _merge(yg, mp, z, x, w_proj_rwkv, w_o, gate_b, g_post, tm):
    t, d = x.shape
    const = lambda *shape: pl.BlockSpec(shape, lambda i: (0,) * len(shape))
    blk = pl.BlockSpec((tm, d), lambda i: (i, 0))
    return pl.pallas_call(
        _merge_kernel,
        grid=(t // tm,),
        in_specs=[blk, blk, pl.BlockSpec((tm, d), lambda i: (i, ZBLK_G1)), blk,
                  const(d, d), const(d, d), const(N_BRANCH, d), const(1, d)],
        out_specs=blk,
        out_shape=jax.ShapeDtypeStruct((t, d), F32),
        compiler_params=_params("parallel"),
        name="merge",
    )(yg, mp, z, x, w_proj_rwkv, w_o, gate_b, g_post)


def _gelu_tanh(x):
    return 0.5 * x * (1.0 + jnp.tanh(0.7978845608028654 * (x + 0.044715 * (x * x * x))))


def _ffn_kernel(x_ref, gpre_ref, wg_ref, wv_ref, cwg_ref, cwv_ref, cbg_ref, cbv_ref, wd_ref,
                gpost_ref, o_ref, hn_ref, acc_ref, carry_ref, *, tm, tiles_per_seq):
    i = pl.program_id(0)
    j = pl.program_id(1)
    first = (i % tiles_per_seq) == 0

    @pl.when(j == 0)
    def _():
        hn_ref[...] = _rms(x_ref[...], gpre_ref[...]).astype(BF16)
        acc_ref[...] = jnp.zeros_like(acc_ref)

    def conv(u, slot, cw_ref, cb_ref):
        prev = jnp.where(first, 0.0, carry_ref[j, slot])
        carry_ref[j, slot] = u[tm - SUBLANES:, :]
        ext = jnp.concatenate([prev, u], axis=0)
        out = cb_ref[...] + cw_ref[0:1, :] * pltpu.roll(ext, 2, axis=0)[SUBLANES:]
        out = out + cw_ref[1:2, :] * pltpu.roll(ext, 1, axis=0)[SUBLANES:]
        return out + cw_ref[2:3, :] * u

    hn = hn_ref[...]
    ug = conv(_dot(hn, wg_ref[...]), 0, cwg_ref, cbg_ref)
    uv = conv(_dot(hn, wv_ref[...]), 1, cwv_ref, cbv_ref)
    f = (_gelu_tanh(ug) * uv).astype(BF16)
    acc_ref[...] += _dot(f, wd_ref[...])

    @pl.when(j == pl.num_programs(1) - 1)
    def _():
        o_ref[...] = x_ref[...] + _rms(acc_ref[...], gpost_ref[...])


def _ffn(x, g_pre, w_up, conv_w, conv_b, w_down, g_post, seq, tm):
    t, d = x.shape
    fc = FFN_COLS
    nf = D_FF // fc
    xblk = pl.BlockSpec((tm, d), lambda i, j: (i, 0))
    row = pl.BlockSpec((1, d), lambda i, j: (0, 0))
    return pl.pallas_call(
        functools.partial(_ffn_kernel, tm=tm, tiles_per_seq=seq // tm),
        grid=(t // tm, nf),
        in_specs=[xblk, row,
                  pl.BlockSpec((d, fc), lambda i, j: (0, j)),
                  pl.BlockSpec((d, fc), lambda i, j: (0, nf + j)),
                  pl.BlockSpec((CONV_WIDTH, fc), lambda i, j: (0, j)),
                  pl.BlockSpec((CONV_WIDTH, fc), lambda i, j: (0, nf + j)),
                  pl.BlockSpec((1, fc), lambda i, j: (0, j)),
                  pl.BlockSpec((1, fc), lambda i, j: (0, nf + j)),
                  pl.BlockSpec((fc, d), lambda i, j: (j, 0)),
                  row],
        out_specs=xblk,
        out_shape=jax.ShapeDtypeStruct((t, d), F32),
        scratch_shapes=[pltpu.VMEM((tm, d), BF16), pltpu.VMEM((tm, d), F32),
                        pltpu.VMEM((nf, 2, SUBLANES, fc), F32)],
        compiler_params=_params("arbitrary", "arbitrary"),
        name="conv_ffn",
    )(x, g_pre, w_up, w_up, conv_w, conv_w, conv_b, conv_b, w_down, g_post)


def _reorder_in_cols(a):
    return jnp.concatenate([a[..., :OFF_RWKV], a[..., OFF_GATE:], a[..., OFF_RWKV:OFF_GATE]], axis=-1)


def _pad_rows(w, before, total):
    return jnp.pad(w, ((before, total - before - w.shape[0]), (0, 0)))


def kernel(x, mem, mem_norm, norm_mix_pre, norm_mix_post, w_in, mu_shift, pool_w, pool_b,
           pool_scale, w_proj_pool, w_mem_kv, w_proj_mem, w0, w_up_decay, a0, w_up_a, w_up_g,
           k_k, k_a, r_k, ln_x_w, ln_x_b, v0, w_down_v, w_up_v, w_proj_rwkv, gate_b, w_o,
           norm_ffn_pre, norm_ffn_post, w_ffn_up, conv_w, conv_b, w_ffn_down):
    batch, seq, d = x.shape
    depth = w_in.shape[0]
    assert d == D_MODEL and seq % 512 == 0 and mem.shape[1] == MEM_LEN
    t = batch * seq
    xf = x.reshape(t, d)
    memf = mem.reshape(batch * MEM_LEN, d)
    row = lambda a: a.reshape(1, -1)
    bf = lambda a: a.astype(BF16)

    v_first = None
    for l in range(depth):
        z = _norm_matmul(xf, row(norm_mix_pre[l]), bf(_reorder_in_cols(w_in[l])), 512, IN_COLS // 2)
        kv = _norm_matmul(memf, row(mem_norm), bf(w_mem_kv[l]), 512, 2 * MEM_WIDTH)
        mp = _branches(z, kv, bf(pool_w[l]), row(pool_b[l]), row(pool_scale[l]), bf(w_proj_pool[l]),
                       bf(w_proj_mem[l]), gate_b[l], batch, seq, 512)

        mu = mu_shift[l]
        vres = None
        if l > 0:
            vres = (v_first, row(v0[l - 1]),
                    bf(jnp.pad(w_down_v[l - 1], ((0, 0), (0, LANES - VRES_LORA)))),
                    bf(_pad_rows(w_up_v[l - 1], 0, LANES)))
        r, lw, k, v, kk, b, g = _prep(
            z, mu[:3 * RWKV_WIDTH].reshape(3, RWKV_WIDTH), row(mu[3 * RWKV_WIDTH:]), row(w0[l]),
            bf(_pad_rows(w_up_decay[l], 0, LANES)), row(a0[l]),
            bf(_pad_rows(w_up_a[l], DECAY_LORA, LANES)), bf(w_up_g[l]), row(k_k[l]), row(k_a[l]),
            vres, batch, seq, 256)
        if l == 0:
            v_first = v
        yg = _scan(r, lw, k, v, kk, b, g, row(r_k[l]), row(ln_x_w[l]), row(ln_x_b[l]), batch, seq)
        xf = _merge(yg, mp, z, xf, bf(w_proj_rwkv[l]), bf(w_o[l]), gate_b[l], row(norm_mix_post[l]), 512)
        xf = _ffn(xf, row(norm_ffn_pre[l]), bf(w_ffn_up[l]), conv_w[l], row(conv_b[l]),
                  bf(w_ffn_down[l]), row(norm_ffn_post[l]), seq, 512)
    return xf.reshape(batch, seq, d)
```

```python
import functools
import math

import jax
import jax.numpy as jnp
from jax import lax
from jax.experimental import pallas as pl
from jax.experimental.pallas import tpu as pltpu

F32 = jnp.float32
BF16 = jnp.bfloat16

D_MODEL = 1024
MEM_LEN = 256
POOL_WIDTH = 512
POOL_GROUP_DIM = 128
POOL_WINDOWS = (2, 4, 8, 16)
POOL_HALO = 16
MEM_HEADS = 4
MEM_HEAD_DIM = 128
MEM_WIDTH = MEM_HEADS * MEM_HEAD_DIM
RWKV_HEAD_DIM = 64
RWKV_WIDTH = D_MODEL
DECAY_LORA = 64
ICLR_LORA = 64
VRES_LORA = 32
GATE_LORA = 128
LORA_COLS = DECAY_LORA + ICLR_LORA + GATE_LORA
RWKV_COLS = 3 * RWKV_WIDTH + LORA_COLS
N_BRANCH = 3
OFF_Q = POOL_WIDTH
OFF_RWKV = OFF_Q + MEM_WIDTH
OFF_GATE = OFF_RWKV + RWKV_COLS
IN_COLS = OFF_GATE + N_BRANCH * D_MODEL
D_FF = 2816
CONV_WIDTH = 3
NORM_EPS = 1e-6
LNX_EPS = 64e-5
L2_EPS = 1e-12

LANES = 128
SUBLANES = 8
VMEM_LIMIT_BYTES = 56 * 1024 * 1024

SCAN_CHUNK = 64
HEADS_PER_SLAB = LANES // RWKV_HEAD_DIM
N_SLABS = RWKV_WIDTH // LANES
FFN_COLS = 256

ZBLK_PQ, ZBLK_G0, ZBLK_G1, ZBLK_G2, ZBLK_R, ZBLK_K, ZBLK_V = range(7)
ZBLK_LORA = 7 * D_MODEL // LORA_COLS


def _params(*sem):
    return pltpu.CompilerParams(dimension_semantics=sem, vmem_limit_bytes=VMEM_LIMIT_BYTES)


def _dot(a, b):
    return jnp.dot(a, b, preferred_element_type=F32)


def _dot_nt(a, b):
    return lax.dot_general(a, b, (((1,), (1,)), ((), ())), preferred_element_type=F32)


def _dot_tn(a, b):
    return lax.dot_general(a, b, (((0,), (0,)), ((), ())), preferred_element_type=F32)


def _sigmoid(x):
    return 1.0 / (1.0 + jnp.exp(-x))


def _rms(x, g):
    ms = jnp.mean(x * x, axis=-1, keepdims=True)
    return x * lax.rsqrt(ms + NORM_EPS) * g


def _split2(x):
    hi = x.astype(BF16)
    lo = (x - hi.astype(F32)).astype(BF16)
    return hi, lo


def _head_sum(x, seg):
    hi, lo = _split2(x)
    return _dot(hi, seg) + _dot(lo, seg)


def _seg_matrix():
    i = lax.broadcasted_iota(jnp.int32, (LANES, LANES), 0) // RWKV_HEAD_DIM
    j = lax.broadcasted_iota(jnp.int32, (LANES, LANES), 1) // RWKV_HEAD_DIM
    return jnp.where(i == j, 1.0, 0.0).astype(BF16)


def _norm_matmul_kernel(x_ref, g_ref, w_ref, o_ref):
    o_ref[...] = _dot(_rms(x_ref[...], g_ref[...]).astype(BF16), w_ref[...])


def _norm_matmul(x, g, w, tm, tn):
    t, d = x.shape
    n = w.shape[1]
    return pl.pallas_call(
        _norm_matmul_kernel,
        grid=(n // tn, t // tm),
        in_specs=[pl.BlockSpec((tm, d), lambda j, i: (i, 0)),
                  pl.BlockSpec((1, d), lambda j, i: (0, 0)),
                  pl.BlockSpec((d, tn), lambda j, i: (0, j))],
        out_specs=pl.BlockSpec((tm, tn), lambda j, i: (i, j)),
        out_shape=jax.ShapeDtypeStruct((t, n), F32),
        compiler_params=_params("parallel", "parallel"),
        name="norm_matmul",
    )(x, g, w)


def _branches_kernel(zpq_ref, halo_ref, zg0_ref, zg2_ref, kv_ref, poolw_ref, poolb_ref,
                     pools_ref, wpp_ref, wpm_ref, gb_ref, o_ref, *, tm):
    i = pl.program_id(1)
    p = zpq_ref[:, :POOL_WIDTH]
    prev = jnp.where(i == 0, 0.0, halo_ref[...])
    pe = jnp.concatenate([prev, p], axis=0)
    pos = lax.broadcasted_iota(jnp.int32, (tm, POOL_GROUP_DIM), 0) + i * tm
    mixed = []
    for gi, win in enumerate(POOL_WINDOWS):
        sl = slice(gi * POOL_GROUP_DIM, (gi + 1) * POOL_GROUP_DIM)
        s = pe[:, sl]
        k = 1
        while k < win:
            s = s + pltpu.roll(s, k, axis=0)
            k *= 2
        cnt = jnp.minimum(pos + 1, win).astype(F32)
        pooled = s[POOL_HALO:] / cnt - p[:, sl]
        m = _dot(pooled.astype(BF16), poolw_ref[gi]) + poolb_ref[:, sl]
        mixed.append((m * pools_ref[:, sl]).astype(BF16))
    y_pool = _dot(jnp.concatenate(mixed, axis=1), wpp_ref[...])

    hd = [slice(h * MEM_HEAD_DIM, (h + 1) * MEM_HEAD_DIM) for h in range(MEM_HEADS)]
    scores = [_dot_nt(zpq_ref[:, POOL_WIDTH + sl.start:POOL_WIDTH + sl.stop].astype(BF16),
                      kv_ref[:, sl].astype(BF16)) * (MEM_HEAD_DIM ** -0.5) for sl in hd]
    probs = []
    for s in scores:
        e = jnp.exp(s - jnp.max(s, axis=-1, keepdims=True))
        probs.append((e / jnp.sum(e, axis=-1, keepdims=True)).astype(BF16))
    heads = [_dot(probs[h], kv_ref[:, MEM_WIDTH + sl.start:MEM_WIDTH + sl.stop].astype(BF16)).astype(BF16)
             for h, sl in enumerate(hd)]
    y_mem = _dot(jnp.concatenate(heads, axis=1), wpm_ref[...])

    g0 = _sigmoid(zg0_ref[...] + gb_ref[0:1, :])
    g2 = _sigmoid(zg2_ref[...] + gb_ref[2:3, :])
    o_ref[...] = g0 * y_pool + g2 * y_mem


def _branches(z, kv, pool_w, pool_b, pool_scale, w_proj_pool, w_proj_mem, gate_b, batch, seq, tm):
    t = z.shape[0]
    nt = seq // tm
    hb = tm // POOL_HALO
    d = D_MODEL
    const = lambda *shape: pl.BlockSpec(shape, lambda b, i: (0,) * len(shape))
    return pl.pallas_call(
        functools.partial(_branches_kernel, tm=tm),
        grid=(batch, nt),
        in_specs=[pl.BlockSpec((tm, d), lambda b, i: (b * nt + i, ZBLK_PQ)),
                  pl.BlockSpec((POOL_HALO, POOL_WIDTH),
                               lambda b, i: (jnp.maximum((b * nt + i) * hb - 1, 0), 0)),
                  pl.BlockSpec((tm, d), lambda b, i: (b * nt + i, ZBLK_G0)),
                  pl.BlockSpec((tm, d), lambda b, i: (b * nt + i, ZBLK_G2)),
                  pl.BlockSpec((MEM_LEN, 2 * MEM_WIDTH), lambda b, i: (b, 0)),
                  const(len(POOL_WINDOWS), POOL_GROUP_DIM, POOL_GROUP_DIM),
                  const(1, POOL_WIDTH), const(1, POOL_WIDTH),
                  const(POOL_WIDTH, d), const(MEM_WIDTH, d), const(N_BRANCH, d)],
        out_specs=pl.BlockSpec((tm, d), lambda b, i: (b * nt + i, 0)),
        out_shape=jax.ShapeDtypeStruct((t, d), F32),
        compiler_params=_params("parallel", "arbitrary"),
        name="branches",
    )(z, z, z, z, kv, pool_w, pool_b, pool_scale, w_proj_pool, w_proj_mem, gate_b)


def _prep_kernel(*refs, tm, has_vres):
    (zr_ref, zk_ref, zv_ref, zl_ref, hr_ref, hk_ref, hv_ref, hl_ref,
     mu_ref, mul_ref, w0_ref, wdec_ref, a0_ref, wa_ref, wg_ref, kk_ref, ka_ref) = refs[:17]
    refs = refs[17:]
    if has_vres:
        vfirst_ref, v0_ref, wdv_ref, wuv_ref = refs[:4]
        refs = refs[4:]
    r_o, lw_o, k_o, v_o, kk_o, b_o, g_o = refs

    first = pl.program_id(1) == 0

    def shifted(z_ref, h_ref, mu):
        z = z_ref[...]
        row = lax.broadcasted_iota(jnp.int32, z.shape, 0)
        last = jnp.where(first, 0.0, h_ref[SUBLANES - 1:SUBLANES, :])
        zp = jnp.where(row == 0, last, pltpu.roll(z, 1, axis=0))
        return z + (zp - z) * mu

    r = shifted(zr_ref, hr_ref, mu_ref[0:1, :])
    k = shifted(zk_ref, hk_ref, mu_ref[1:2, :])
    v = shifted(zv_ref, hv_ref, mu_ref[2:3, :])
    lo = shifted(zl_ref, hl_ref, mul_ref[...])
    dwa = lo[:, :LANES]
    dg = lo[:, LANES:]

    if has_vres:
        low = _dot(v.astype(BF16), wdv_ref[...])
        vg = _sigmoid(v0_ref[...] + _dot(low.astype(BF16), wuv_ref[...]))
        v = v + (vfirst_ref[...] - v) * vg

    u = w0_ref[...] + _dot(jnp.tanh(dwa).astype(BF16), wdec_ref[...])
    lw_o[...] = -math.exp(-0.5) * _sigmoid(u)
    alpha = _sigmoid(a0_ref[...] + _dot(dwa.astype(BF16), wa_ref[...]))
    g_o[...] = _dot(_sigmoid(dg).astype(BF16), wg_ref[...])

    seg = _seg_matrix()
    kk = k * kk_ref[...]
    kk2 = kk * kk
    ss = jnp.concatenate([_head_sum(kk2[:, s * LANES:(s + 1) * LANES], seg) for s in range(N_SLABS)], axis=1)
    kk = kk * lax.rsqrt(ss + L2_EPS)
    r_o[...] = r
    k_o[...] = k * (1.0 + (alpha - 1.0) * ka_ref[...])
    v_o[...] = v
    kk_o[...] = kk
    b_o[...] = kk * alpha


def _prep(z, mu3, mul, w0, wdec, a0, wa, wg, k_k, k_a, vres, batch, seq, tm):
    t = z.shape[0]
    nt = seq // tm
    hb = tm // SUBLANES
    d = D_MODEL
    const = lambda *shape: pl.BlockSpec(shape, lambda b, i: (0,) * len(shape))
    main = lambda w, blk: pl.BlockSpec((tm, w), lambda b, i: (b * nt + i, blk))
    halo = lambda w, blk: pl.BlockSpec(
        (SUBLANES, w), lambda b, i: (jnp.maximum((b * nt + i) * hb - 1, 0), blk))
    in_specs = [main(d, ZBLK_R), main(d, ZBLK_K), main(d, ZBLK_V), main(LORA_COLS, ZBLK_LORA),
                halo(d, ZBLK_R), halo(d, ZBLK_K), halo(d, ZBLK_V), halo(LORA_COLS, ZBLK_LORA),
                const(3, d), const(1, LORA_COLS), const(1, d), const(LANES, d), const(1, d),
                const(LANES, d), const(GATE_LORA, d), const(1, d), const(1, d)]
    args = [z] * 8 + [mu3, mul, w0, wdec, a0, wa, wg, k_k, k_a]
    if vres is not None:
        v_first, v0, wdv, wuv = vres
        in_specs += [pl.BlockSpec((tm, d), lambda b, i: (b * nt + i, 0)),
                     const(1, d), const(d, LANES), const(LANES, d)]
        args += [v_first, v0, wdv, wuv]
    out_spec = pl.BlockSpec((tm, d), lambda b, i: (b * nt + i, 0))
    return pl.pallas_call(
        functools.partial(_prep_kernel, tm=tm, has_vres=vres is not None),
        grid=(batch, nt),
        in_specs=in_specs,
        out_specs=[out_spec] * 7,
        out_shape=[jax.ShapeDtypeStruct((t, d), F32)] * 7,
        compiler_params=_params("parallel", "arbitrary"),
        name="rwkv_prep",
    )(*args)


def _stack_heads(x, m1):
    return jnp.concatenate([jnp.where(m1, x, 0.0), jnp.where(m1, 0.0, x)], axis=0)


def _scan_kernel(r_ref, lw_ref, k_ref, v_ref, kk_ref, b_ref, g_ref, rk_ref, lnw_ref, lnb_ref,
                 o_ref, h_ref):
    c = SCAN_CHUNK
    c2 = HEADS_PER_SLAB * c

    @pl.when(pl.program_id(1) == 0)
    def _():
        h_ref[...] = jnp.zeros_like(h_ref)

    ti = lax.broadcasted_iota(jnp.int32, (c, c), 0)
    tj = lax.broadcasted_iota(jnp.int32, (c, c), 1)
    ltri = jnp.where(ti >= tj, 1.0, 0.0).astype(BF16)
    lw = lw_ref[...]
    hi = lw.astype(BF16)
    r1 = lw - hi.astype(F32)
    mid = r1.astype(BF16)
    lo = (r1 - mid.astype(F32)).astype(BF16)
    cum = _dot(ltri, hi) + _dot(ltri, mid) + _dot(ltri, lo)

    m1 = lax.broadcasted_iota(jnp.int32, (c, LANES), 1) < RWKV_HEAD_DIM
    si = lax.broadcasted_iota(jnp.int32, (c2, c2), 0)
    sj = lax.broadcasted_iota(jnp.int32, (c2, c2), 1)
    strict = si > sj
    incl = si >= sj
    eye = jnp.where(si == sj, 1.0, 0.0)
    seg = _seg_matrix()
    inv_n = 1.0 / RWKV_HEAD_DIM

    slabs = range(N_SLABS)
    sls = [slice(s * LANES, (s + 1) * LANES) for s in slabs]
    ar, d_ak, d_r, tinv, apow, bk, decay_c = [], [], [], [], [], [], []
    for sl in sls:
        cum_s = cum[:, sl]
        last = cum_s[c - 1:c, :]
        p_inv = jnp.exp(-cum_s)
        p_end = jnp.exp(last - cum_s)
        kx = k_ref[:, sl]
        bb = b_ref[:, sl]
        a_t = -kk_ref[:, sl] * jnp.exp(cum_s - lw[:, sl])
        r_t = r_ref[:, sl] * jnp.exp(cum_s)
        xl = jnp.concatenate([_stack_heads(a_t, m1), _stack_heads(r_t, m1)], axis=0).astype(BF16)
        xr = jnp.concatenate([_stack_heads(bb * p_inv, m1), _stack_heads(kx * p_inv, m1)],
                             axis=0).astype(BF16)
        gm = _dot_nt(xl, xr)
        d_ab = jnp.where(strict, gm[:c2, :c2], 0.0)
        d_ak.append(jnp.where(strict, gm[:c2, c2:], 0.0).astype(BF16))
        d_r.append(jnp.concatenate([jnp.where(incl, gm[c2:, :c2], 0.0),
                                    jnp.where(incl, gm[c2:, c2:], 0.0)], axis=1).astype(BF16))
        tinv.append(eye + d_ab)
        apow.append(d_ab.astype(BF16))
        ar.append(jnp.concatenate([a_t, r_t], axis=0).astype(BF16))
        bk.append(jnp.concatenate([_stack_heads(bb * p_end, m1), _stack_heads(kx * p_end, m1)],
                                  axis=0).astype(BF16))
        decay_c.append(jnp.exp(last))

    n = 2
    while n < c:
        for s in slabs:
            apow[s] = _dot(apow[s], apow[s]).astype(BF16)
        for s in slabs:
            tinv[s] = tinv[s] + _dot(tinv[s].astype(BF16), apow[s])
        n *= 2

    hs = [h_ref[s] for s in slabs]
    xh = [_dot_nt(ar[s], hs[s].astype(BF16)) for s in slabs]
    v_st = [_stack_heads(v_ref[:, sl], m1) for sl in sls]
    z = [_stack_heads(xh[s][:c], m1) + _dot(d_ak[s], v_st[s].astype(BF16)) for s in slabs]
    u = [_dot(tinv[s].astype(BF16), z[s].astype(BF16)) for s in slabs]
    uv = [jnp.concatenate([u[s], v_st[s]], axis=0).astype(BF16) for s in slabs]
    y_st = [_dot(d_r[s], uv[s]) for s in slabs]
    for s in slabs:
        h_ref[s] = hs[s] * decay_c[s] + _dot_tn(uv[s], bk[s])
    y = [y_st[s][:c] + y_st[s][c:] + xh[s][c:] for s in slabs]

    mu = [_head_sum(y[s], seg) * inv_n for s in slabs]
    dlt = [y[s] - mu[s] for s in slabs]
    var = [_head_sum(dlt[s] * dlt[s], seg) * inv_n for s in slabs]
    bonus = [_head_sum(r_ref[:, sl] * k_ref[:, sl] * rk_ref[:, sl], seg) for sl in sls]
    for s, sl in enumerate(sls):
        yn = dlt[s] * lax.rsqrt(var[s] + LNX_EPS) * lnw_ref[:, sl] + lnb_ref[:, sl]
        o_ref[:, sl] = ((yn + bonus[s] * v_ref[:, sl]) * g_ref[:, sl]).astype(BF16)


def _scan(r, lw, k, v, kk, b, g, r_k, ln_w, ln_b, batch, seq):
    t, d = r.shape
    c = SCAN_CHUNK
    nc = seq // c
    blk = pl.BlockSpec((c, d), lambda bi, ci: (bi * nc + ci, 0))
    row = pl.BlockSpec((1, d), lambda bi, ci: (0, 0))
    return pl.pallas_call(
        _scan_kernel,
        grid=(batch, nc),
        in_specs=[blk] * 7 + [row] * 3,
        out_specs=blk,
        out_shape=jax.ShapeDtypeStruct((t, d), BF16),
        scratch_shapes=[pltpu.VMEM((N_SLABS, LANES, LANES), F32)],
        compiler_params=_params("parallel", "arbitrary"),
        name="rwkv_scan",
    )(r, lw, k, v, kk, b, g, r_k, ln_w, ln_b)


def _merge_kernel(yg_ref, mp_ref, zg1_ref, x_ref, wpr_ref, wo_ref, gb_ref, gn_ref, o_ref):
    y_rwkv = _dot(yg_ref[...], wpr_ref[...])
    merged = mp_ref[...] + _sigmoid(zg1_ref[...] + gb_ref[1:2, :]) * y_rwkv
    o = _dot(merged.astype(BF16), wo_ref[...])
    o_ref[...] = x_ref[...] + _rms(o, gn_ref[...])


def _merge(yg, mp, z, x, w_proj_rwkv, w_o, gate_b, g_post, tm):
    t, d = x.shape
    const = lambda *shape: pl.BlockSpec(shape, lambda i: (0,) * len(shape))
    blk = pl.BlockSpec((tm, d), lambda i: (i, 0))
    return pl.pallas_call(
        _merge_kernel,
        grid=(t // tm,),
        in_specs=[blk, blk, pl.BlockSpec((tm, d), lambda i: (i, ZBLK_G1)), blk,
                  const(d, d), const(d, d), const(N_BRANCH, d), const(1, d)],
        out_specs=blk,
        out_shape=jax.ShapeDtypeStruct((t, d), F32),
        compiler_params=_params("parallel"),
        name="merge",
    )(yg, mp, z, x, w_proj_rwkv, w_o, gate_b, g_post)


GELU_C0 = math.sqrt(2.0 / math.pi)
GELU_C1 = GELU_C0 * 0.044715


def _gelu_tanh(x):
    hx = 0.5 * x
    return hx + hx * jnp.tanh(x * (GELU_C0 + GELU_C1 * (x * x)))


def _ffn_kernel(x_ref, gpre_ref, wup_ref, cw_ref, cb_ref, wd_ref, gpost_ref, o_ref,
                ubuf_ref, f_ref, carry_ref, *, tm, tiles_per_seq):
    first = (pl.program_id(0) % tiles_per_seq) == 0
    fc = FFN_COLS
    hn = _rms(x_ref[...], gpre_ref[...]).astype(BF16)
    for j in range(D_FF // fc):
        halves = []
        for half in range(2):
            cols = slice(half * D_FF + j * fc, half * D_FF + (j + 1) * fc)
            buf = ubuf_ref.at[2 * (j % 2) + half]
            u = _dot(hn, wup_ref[:, cols])
            buf[0:SUBLANES, :] = jnp.where(first, 0.0, carry_ref[:, cols])
            buf[SUBLANES:, :] = u
            carry_ref[:, cols] = u[tm - SUBLANES:, :]
            out = cb_ref[:, cols] + cw_ref[0:1, cols] * buf[SUBLANES - 2:SUBLANES - 2 + tm, :]
            out = out + cw_ref[1:2, cols] * buf[SUBLANES - 1:SUBLANES - 1 + tm, :]
            halves.append(out + cw_ref[2:3, cols] * u)
        f_ref[:, j * fc:(j + 1) * fc] = (_gelu_tanh(halves[0]) * halves[1]).astype(BF16)
    acc = _dot(f_ref[...], wd_ref[...])
    o_ref[...] = x_ref[...] + _rms(acc, gpost_ref[...])


def _ffn(x, g_pre, w_up, conv_w, conv_b, w_down, g_post, seq, tm):
    t, d = x.shape
    xblk = pl.BlockSpec((tm, d), lambda i: (i, 0))
    const = lambda *shape: pl.BlockSpec(shape, lambda i: (0,) * len(shape))
    return pl.pallas_call(
        functools.partial(_ffn_kernel, tm=tm, tiles_per_seq=seq // tm),
        grid=(t // tm,),
        in_specs=[xblk, const(1, d), const(d, 2 * D_FF), const(CONV_WIDTH, 2 * D_FF),
                  const(1, 2 * D_FF), const(D_FF, d), const(1, d)],
        out_specs=xblk,
        out_shape=jax.ShapeDtypeStruct((t, d), F32),
        scratch_shapes=[pltpu.VMEM((4, SUBLANES + tm, FFN_COLS), F32),
                        pltpu.VMEM((tm, D_FF), BF16),
                        pltpu.VMEM((SUBLANES, 2 * D_FF), F32)],
        compiler_params=_params("arbitrary"),
        name="conv_ffn",
    )(x, g_pre, w_up, conv_w, conv_b, w_down, g_post)


def _reorder_in_cols(a):
    return jnp.concatenate([a[..., :OFF_RWKV], a[..., OFF_GATE:], a[..., OFF_RWKV:OFF_GATE]], axis=-1)


def _pad_rows(w, before, total):
    return jnp.pad(w, ((before, total - before - w.shape[0]), (0, 0)))


def kernel(x, mem, mem_norm, norm_mix_pre, norm_mix_post, w_in, mu_shift, pool_w, pool_b,
           pool_scale, w_proj_pool, w_mem_kv, w_proj_mem, w0, w_up_decay, a0, w_up_a, w_up_g,
           k_k, k_a, r_k, ln_x_w, ln_x_b, v0, w_down_v, w_up_v, w_proj_rwkv, gate_b, w_o,
           norm_ffn_pre, norm_ffn_post, w_ffn_up, conv_w, conv_b, w_ffn_down):
    batch, seq, d = x.shape
    depth = w_in.shape[0]
    assert d == D_MODEL and seq % 512 == 0 and mem.shape[1] == MEM_LEN
    t = batch * seq
    xf = x.reshape(t, d)
    memf = mem.reshape(batch * MEM_LEN, d)
    row = lambda a: a.reshape(1, -1)
    bf = lambda a: a.astype(BF16)

    v_first = None
    for l in range(depth):
        z = _norm_matmul(xf, row(norm_mix_pre[l]), bf(_reorder_in_cols(w_in[l])), 512, IN_COLS // 2)
        kv = _norm_matmul(memf, row(mem_norm), bf(w_mem_kv[l]), 512, 2 * MEM_WIDTH)
        mp = _branches(z, kv, bf(pool_w[l]), row(pool_b[l]), row(pool_scale[l]), bf(w_proj_pool[l]),
                       bf(w_proj_mem[l]), gate_b[l], batch, seq, 512)

        mu = mu_shift[l]
        vres = None
        if l > 0:
            vres = (v_first, row(v0[l - 1]),
                    bf(jnp.pad(w_down_v[l - 1], ((0, 0), (0, LANES - VRES_LORA)))),
                    bf(_pad_rows(w_up_v[l - 1], 0, LANES)))
        r, lw, k, v, kk, b, g = _prep(
            z, mu[:3 * RWKV_WIDTH].reshape(3, RWKV_WIDTH), row(mu[3 * RWKV_WIDTH:]), row(w0[l]),
            bf(_pad_rows(w_up_decay[l], 0, LANES)), row(a0[l]),
            bf(_pad_rows(w_up_a[l], DECAY_LORA, LANES)), bf(w_up_g[l]), row(k_k[l]), row(k_a[l]),
            vres, batch, seq, 256)
        if l == 0:
            v_first = v
        yg = _scan(r, lw, k, v, kk, b, g, row(r_k[l]), row(ln_x_w[l]), row(ln_x_b[l]), batch, seq)
        xf = _merge(yg, mp, z, xf, bf(w_proj_rwkv[l]), bf(w_o[l]), gate_b[l], row(norm_mix_post[l]), 512)
        xf = _ffn(xf, row(norm_ffn_pre[l]), bf(w_ffn_up[l]), conv_w[l], row(conv_b[l]),
                  bf(w_ffn_down[l]), row(norm_ffn_post[l]), seq, 512)
    return xf.reshape(batch, seq, d)
```

```python
import functools
import math

import jax
import jax.numpy as jnp
from jax import lax
from jax.experimental import pallas as pl
from jax.experimental.pallas import tpu as pltpu

F32 = jnp.float32
BF16 = jnp.bfloat16

D_MODEL = 1024
MEM_LEN = 256
POOL_WIDTH = 512
POOL_GROUP_DIM = 128
POOL_WINDOWS = (2, 4, 8, 16)
POOL_HALO = 16
MEM_HEADS = 4
MEM_HEAD_DIM = 128
MEM_WIDTH = MEM_HEADS * MEM_HEAD_DIM
RWKV_HEAD_DIM = 64
RWKV_WIDTH = D_MODEL
DECAY_LORA = 64
ICLR_LORA = 64
VRES_LORA = 32
GATE_LORA = 128
LORA_COLS = DECAY_LORA + ICLR_LORA + GATE_LORA
RWKV_COLS = 3 * RWKV_WIDTH + LORA_COLS
N_BRANCH = 3
OFF_Q = POOL_WIDTH
OFF_RWKV = OFF_Q + MEM_WIDTH
OFF_GATE = OFF_RWKV + RWKV_COLS
IN_COLS = OFF_GATE + N_BRANCH * D_MODEL
D_FF = 2816
CONV_WIDTH = 3
NORM_EPS = 1e-6
LNX_EPS = 64e-5
L2_EPS = 1e-12

LANES = 128
SUBLANES = 8
VMEM_LIMIT_BYTES = 56 * 1024 * 1024

SCAN_CHUNK = 64
SCAN_SEQS = 2
HEADS_PER_SLAB = LANES // RWKV_HEAD_DIM
N_SLABS = RWKV_WIDTH // LANES
FFN_COLS = 256

ZBLK_PQ, ZBLK_G0, ZBLK_G1, ZBLK_G2 = range(4)


def _params(*sem):
    return pltpu.CompilerParams(dimension_semantics=sem, vmem_limit_bytes=VMEM_LIMIT_BYTES)


def _dot(a, b):
    return jnp.dot(a, b, preferred_element_type=F32)


def _dot_nt(a, b):
    return lax.dot_general(a, b, (((1,), (1,)), ((), ())), preferred_element_type=F32)


def _dot_tn(a, b):
    return lax.dot_general(a, b, (((0,), (0,)), ((), ())), preferred_element_type=F32)


def _sigmoid(x):
    return 1.0 / (1.0 + jnp.exp(-x))


def _rms(x, g):
    ms = jnp.mean(x * x, axis=-1, keepdims=True)
    return x * lax.rsqrt(ms + NORM_EPS) * g


def _split2(x):
    hi = x.astype(BF16)
    lo = (x - hi.astype(F32)).astype(BF16)
    return hi, lo


def _head_sums(xs, seg):
    rows = xs[0].shape[0]
    parts = [p for x in xs for p in _split2(x)]
    res = _dot(jnp.concatenate(parts, axis=0), seg)
    return [res[2 * i * rows:(2 * i + 1) * rows] + res[(2 * i + 1) * rows:(2 * i + 2) * rows]
            for i in range(len(xs))]


def _seg_matrix():
    i = lax.broadcasted_iota(jnp.int32, (LANES, LANES), 0) // RWKV_HEAD_DIM
    j = lax.broadcasted_iota(jnp.int32, (LANES, LANES), 1) // RWKV_HEAD_DIM
    return jnp.where(i == j, 1.0, 0.0).astype(BF16)


def _norm_matmul_kernel(x_ref, g_ref, w_ref, o_ref):
    o_ref[...] = _dot(_rms(x_ref[...], g_ref[...]).astype(BF16), w_ref[...])


def _norm_matmul(x, g, w, tm, tn):
    t, d = x.shape
    n = w.shape[1]
    return pl.pallas_call(
        _norm_matmul_kernel,
        grid=(n // tn, t // tm),
        in_specs=[pl.BlockSpec((tm, d), lambda j, i: (i, 0)),
                  pl.BlockSpec((1, d), lambda j, i: (0, 0)),
                  pl.BlockSpec((d, tn), lambda j, i: (0, j))],
        out_specs=pl.BlockSpec((tm, tn), lambda j, i: (i, j)),
        out_shape=jax.ShapeDtypeStruct((t, n), F32),
        compiler_params=_params("parallel", "parallel"),
        name="norm_matmul",
    )(x, g, w)


def _branches_kernel(zpq_ref, halo_ref, zg0_ref, zg2_ref, kv_ref, poolw_ref, poolb_ref,
                     pools_ref, wpp_ref, wpm_ref, gb_ref, o_ref, *, tm):
    i = pl.program_id(1)
    p = zpq_ref[:, :POOL_WIDTH]
    prev = jnp.where(i == 0, 0.0, halo_ref[...])
    pe = jnp.concatenate([prev, p], axis=0)
    pos = lax.broadcasted_iota(jnp.int32, (tm, POOL_GROUP_DIM), 0) + i * tm
    mixed = []
    for gi, win in enumerate(POOL_WINDOWS):
        sl = slice(gi * POOL_GROUP_DIM, (gi + 1) * POOL_GROUP_DIM)
        s = pe[:, sl]
        k = 1
        while k < win:
            s = s + pltpu.roll(s, k, axis=0)
            k *= 2
        cnt = jnp.minimum(pos + 1, win).astype(F32)
        pooled = s[POOL_HALO:] / cnt - p[:, sl]
        m = _dot(pooled.astype(BF16), poolw_ref[gi]) + poolb_ref[:, sl]
        mixed.append((m * pools_ref[:, sl]).astype(BF16))
    y_pool = _dot(jnp.concatenate(mixed, axis=1), wpp_ref[...])

    hd = [slice(h * MEM_HEAD_DIM, (h + 1) * MEM_HEAD_DIM) for h in range(MEM_HEADS)]
    scores = [_dot_nt(zpq_ref[:, POOL_WIDTH + sl.start:POOL_WIDTH + sl.stop].astype(BF16),
                      kv_ref[:, sl].astype(BF16)) * (MEM_HEAD_DIM ** -0.5) for sl in hd]
    probs = []
    for s in scores:
        e = jnp.exp(s - jnp.max(s, axis=-1, keepdims=True))
        probs.append((e / jnp.sum(e, axis=-1, keepdims=True)).astype(BF16))
    heads = [_dot(probs[h], kv_ref[:, MEM_WIDTH + sl.start:MEM_WIDTH + sl.stop].astype(BF16)).astype(BF16)
             for h, sl in enumerate(hd)]
    y_mem = _dot(jnp.concatenate(heads, axis=1), wpm_ref[...])

    g0 = _sigmoid(zg0_ref[...] + gb_ref[0:1, :])
    g2 = _sigmoid(zg2_ref[...] + gb_ref[2:3, :])
    o_ref[...] = g0 * y_pool + g2 * y_mem


def _branches(z, kv, pool_w, pool_b, pool_scale, w_proj_pool, w_proj_mem, gate_b, batch, seq, tm):
    t = z.shape[0]
    nt = seq // tm
    hb = tm // POOL_HALO
    d = D_MODEL
    const = lambda *shape: pl.BlockSpec(shape, lambda b, i: (0,) * len(shape))
    return pl.pallas_call(
        functools.partial(_branches_kernel, tm=tm),
        grid=(batch, nt),
        in_specs=[pl.BlockSpec((tm, d), lambda b, i: (b * nt + i, ZBLK_PQ)),
                  pl.BlockSpec((POOL_HALO, POOL_WIDTH),
                               lambda b, i: (jnp.maximum((b * nt + i) * hb - 1, 0), 0)),
                  pl.BlockSpec((tm, d), lambda b, i: (b * nt + i, ZBLK_G0)),
                  pl.BlockSpec((tm, d), lambda b, i: (b * nt + i, ZBLK_G2)),
                  pl.BlockSpec((MEM_LEN, 2 * MEM_WIDTH), lambda b, i: (b, 0)),
                  const(len(POOL_WINDOWS), POOL_GROUP_DIM, POOL_GROUP_DIM),
                  const(1, POOL_WIDTH), const(1, POOL_WIDTH),
                  const(POOL_WIDTH, d), const(MEM_WIDTH, d), const(N_BRANCH, d)],
        out_specs=pl.BlockSpec((tm, d), lambda b, i: (b * nt + i, 0)),
        out_shape=jax.ShapeDtypeStruct((t, d), F32),
        compiler_params=_params("parallel", "arbitrary"),
        name="branches",
    )(z, z, z, z, kv, pool_w, pool_b, pool_scale, w_proj_pool, w_proj_mem, gate_b)


def _rwkv_in_kernel(*refs, tm, tiles_per_seq, has_vres):
    (x_ref, gn_ref, w_ref, mu_ref, w0_ref, wdec_ref, a0_ref, wa_ref, wg_ref, kk_ref, ka_ref) = refs[:11]
    refs = refs[11:]
    if has_vres:
        vfirst_ref, v0_ref, wdv_ref, wuv_ref = refs[:4]
        refs = refs[4:]
    r_o, lw_o, k_o, v_o, kk_o, b_o, g_o, carry_ref = refs
    d = RWKV_WIDTH

    z = _dot(_rms(x_ref[...], gn_ref[...]).astype(BF16), w_ref[...])
    first = (pl.program_id(0) % tiles_per_seq) == 0
    last = jnp.where(first, 0.0, carry_ref[SUBLANES - 1:SUBLANES, :])
    carry_ref[...] = z[tm - SUBLANES:, :]
    row = lax.broadcasted_iota(jnp.int32, z.shape, 0)
    zp = jnp.where(row == 0, last, pltpu.roll(z, 1, axis=0))
    z = z + (zp - z) * mu_ref[...]
    r = z[:, :d]
    k = z[:, d:2 * d]
    v = z[:, 2 * d:3 * d]
    dwa = z[:, 3 * d:3 * d + LANES]
    dg = z[:, 3 * d + LANES:]

    if has_vres:
        low = _dot(v.astype(BF16), wdv_ref[...])
        vg = _sigmoid(v0_ref[...] + _dot(low.astype(BF16), wuv_ref[...]))
        v = v + (vfirst_ref[...] - v) * vg

    u = w0_ref[...] + _dot(jnp.tanh(dwa).astype(BF16), wdec_ref[...])
    lw_o[...] = -math.exp(-0.5) * _sigmoid(u)
    alpha = _sigmoid(a0_ref[...] + _dot(dwa.astype(BF16), wa_ref[...]))
    g_o[...] = _dot(_sigmoid(dg).astype(BF16), wg_ref[...])

    seg = _seg_matrix()
    kk = k * kk_ref[...]
    kk2 = kk * kk
    ss = jnp.concatenate(_head_sums([kk2[:, s * LANES:(s + 1) * LANES] for s in range(N_SLABS)], seg), axis=1)
    kk = kk * lax.rsqrt(ss + L2_EPS)
    r_o[...] = r
    k_o[...] = k * (1.0 + (alpha - 1.0) * ka_ref[...])
    v_o[...] = v
    kk_o[...] = kk
    b_o[...] = kk * alpha


def _rwkv_in(x, g_pre, w, mu, w0, wdec, a0, wa, wg, k_k, k_a, vres, seq, tm):
    t, d = x.shape
    const = lambda *shape: pl.BlockSpec(shape, lambda i: (0,) * len(shape))
    blk = pl.BlockSpec((tm, d), lambda i: (i, 0))
    in_specs = [blk, const(1, d), const(d, RWKV_COLS), const(1, RWKV_COLS), const(1, d),
                const(LANES, d), const(1, d), const(LANES, d), const(GATE_LORA, d), const(1, d),
                const(1, d)]
    args = [x, g_pre, w, mu, w0, wdec, a0, wa, wg, k_k, k_a]
    if vres is not None:
        in_specs += [blk, const(1, d), const(d, LANES), const(LANES, d)]
        args += list(vres)
    return pl.pallas_call(
        functools.partial(_rwkv_in_kernel, tm=tm, tiles_per_seq=seq // tm, has_vres=vres is not None),
        grid=(t // tm,),
        in_specs=in_specs,
        out_specs=[blk] * 7,
        out_shape=[jax.ShapeDtypeStruct((t, d), F32)] * 7,
        scratch_shapes=[pltpu.VMEM((SUBLANES, RWKV_COLS), F32)],
        compiler_params=_params("arbitrary"),
        name="rwkv_in",
    )(*args)


def _stack_heads(x, m1):
    return jnp.concatenate([jnp.where(m1, x, 0.0), jnp.where(m1, 0.0, x)], axis=0)


def _scan_kernel(r_ref, lw_ref, k_ref, v_ref, kk_ref, b_ref, g_ref, rk_ref, lnw_ref, lnb_ref,
                 o_ref, h_ref):
    c = SCAN_CHUNK
    c2 = HEADS_PER_SLAB * c

    @pl.when(pl.program_id(1) == 0)
    def _():
        h_ref[...] = jnp.zeros_like(h_ref)

    lane = lax.broadcasted_iota(jnp.int32, (c, LANES), 1)
    m1 = lane < RWKV_HEAD_DIM
    ti = lax.broadcasted_iota(jnp.int32, (c, LANES), 0)
    tj = lane % RWKV_HEAD_DIM
    strict = ti > tj
    incl = ti >= tj
    eye = jnp.where(ti == tj, 1.0, 0.0)
    seg = _seg_matrix()
    inv_n = 1.0 / RWKV_HEAD_DIM
    ltri = jnp.where(lax.broadcasted_iota(jnp.int32, (c, c), 0)
                     >= lax.broadcasted_iota(jnp.int32, (c, c), 1), 1.0, 0.0).astype(BF16)

    units = [(j, slice(s * LANES, (s + 1) * LANES)) for j in range(SCAN_SEQS) for s in range(N_SLABS)]
    nu = range(len(units))

    lws, cums = [], []
    for j in range(SCAN_SEQS):
        lw = lw_ref[j]
        hi = lw.astype(BF16)
        r1 = lw - hi.astype(F32)
        mid = r1.astype(BF16)
        lo = (r1 - mid.astype(F32)).astype(BF16)
        lws.append(lw)
        cums.append(_dot(ltri, hi) + _dot(ltri, mid) + _dot(ltri, lo))

    ar, d_ak, d_r, tinv, apow, bk, decay_c = [], [], [], [], [], [], []
    for j, sl in units:
        cum_s = cums[j][:, sl]
        last = cum_s[c - 1:c, :]
        p_inv = jnp.exp(-cum_s)
        p_end = jnp.exp(last - cum_s)
        kx = k_ref[j, :, sl]
        bb = b_ref[j, :, sl]
        a_t = -kk_ref[j, :, sl] * jnp.exp(cum_s - lws[j][:, sl])
        r_t = r_ref[j, :, sl] * jnp.exp(cum_s)
        ar_s = jnp.concatenate([a_t, r_t], axis=0).astype(BF16)
        xr = jnp.concatenate([_stack_heads(bb * p_inv, m1), _stack_heads(kx * p_inv, m1)],
                             axis=0).astype(BF16)
        gm = _dot_nt(ar_s, xr)
        d_ab = jnp.where(strict, gm[:c, :c2], 0.0)
        d_ak.append(jnp.where(strict, gm[:c, c2:], 0.0).astype(BF16))
        d_r.append(jnp.concatenate([jnp.where(incl, gm[c:, :c2], 0.0),
                                    jnp.where(incl, gm[c:, c2:], 0.0)], axis=1).astype(BF16))
        tinv.append(eye + d_ab)
        apow.append(d_ab.astype(BF16))
        ar.append(ar_s)
        bk.append(jnp.concatenate([_stack_heads(bb * p_end, m1), _stack_heads(kx * p_end, m1)],
                                  axis=0).astype(BF16))
        decay_c.append(jnp.exp(last))

    for i in nu:
        apow[i] = _dot(apow[i], _stack_heads(apow[i], m1)).astype(BF16)
    n = 4
    while n < c:
        for i in nu:
            both = _dot(jnp.concatenate([tinv[i].astype(BF16), apow[i]], axis=0),
                        _stack_heads(apow[i], m1))
            tinv[i] = tinv[i] + both[:c]
            apow[i] = both[c:].astype(BF16)
        n *= 2
    for i in nu:
        tinv[i] = (tinv[i] + _dot(tinv[i].astype(BF16), _stack_heads(apow[i], m1))).astype(BF16)

    hs = [h_ref[i] for i in nu]
    xh = [_dot_nt(ar[i], hs[i].astype(BF16)) for i in nu]
    v_st = [_stack_heads(v_ref[j, :, sl], m1).astype(BF16) for j, sl in units]
    z = [xh[i][:c] + _dot(d_ak[i], v_st[i]) for i in nu]
    u = [_dot(tinv[i], _stack_heads(z[i], m1).astype(BF16)) for i in nu]
    uv = [jnp.concatenate([_stack_heads(u[i], m1).astype(BF16), v_st[i]], axis=0) for i in nu]
    y = [xh[i][c:] + _dot(d_r[i], uv[i]) for i in nu]
    for i in nu:
        h_ref[i] = hs[i] * decay_c[i] + _dot_tn(uv[i], bk[i])

    bonus = _head_sums([r_ref[j, :, sl] * k_ref[j, :, sl] * rk_ref[:, sl] for j, sl in units], seg)
    mu = _head_sums(y, seg)
    dlt = [y[i] - mu[i] * inv_n for i in nu]
    var = _head_sums([dlt[i] * dlt[i] for i in nu], seg)
    for i, (j, sl) in enumerate(units):
        yn = dlt[i] * lax.rsqrt(var[i] * inv_n + LNX_EPS) * lnw_ref[:, sl] + lnb_ref[:, sl]
        o_ref[j, :, sl] = ((yn + bonus[i] * v_ref[j, :, sl]) * g_ref[j, :, sl]).astype(BF16)


def _scan(r, lw, k, v, kk, b, g, r_k, ln_w, ln_b, batch, seq):
    t, d = r.shape
    c = SCAN_CHUNK
    as_seqs = lambda a: a.reshape(batch, seq, d)
    blk = pl.BlockSpec((SCAN_SEQS, c, d), lambda bi, ci: (bi, ci, 0))
    row = pl.BlockSpec((1, d), lambda bi, ci: (0, 0))
    out = pl.pallas_call(
        _scan_kernel,
        grid=(batch // SCAN_SEQS, seq // c),
        in_specs=[blk] * 7 + [row] * 3,
        out_specs=blk,
        out_shape=jax.ShapeDtypeStruct((batch, seq, d), BF16),
        scratch_shapes=[pltpu.VMEM((SCAN_SEQS * N_SLABS, LANES, LANES), F32)],
        compiler_params=_params("parallel", "arbitrary"),
        name="rwkv_scan",
    )(*(as_seqs(a) for a in (r, lw, k, v, kk, b, g)), r_k, ln_w, ln_b)
    return out.reshape(t, d)


def _merge_kernel(yg_ref, mp_ref, zg1_ref, x_ref, wpr_ref, wo_ref, gb_ref, gn_ref, o_ref):
    y_rwkv = _dot(yg_ref[...], wpr_ref[...])
    merged = mp_ref[...] + _sigmoid(zg1_ref[...] + gb_ref[1:2, :]) * y_rwkv
    o = _dot(merged.astype(BF16), wo_ref[...])
    o_ref[...] = x_ref[...] + _rms(o, gn_ref[...])


def _merge(yg, mp, z, x, w_proj_rwkv, w_o, gate_b, g_post, tm):
    t, d = x.shape
    const = lambda *shape: pl.BlockSpec(shape, lambda i: (0,) * len(shape))
    blk = pl.BlockSpec((tm, d), lambda i: (i, 0))
    return pl.pallas_call(
        _merge_kernel,
        grid=(t // tm,),
        in_specs=[blk, blk, pl.BlockSpec((tm, d), lambda i: (i, ZBLK_G1)), blk,
                  const(d, d), const(d, d), const(N_BRANCH, d), const(1, d)],
        out_specs=blk,
        out_shape=jax.ShapeDtypeStruct((t, d), F32),
        compiler_params=_params("parallel"),
        name="merge",
    )(yg, mp, z, x, w_proj_rwkv, w_o, gate_b, g_post)


GELU_C0 = math.sqrt(2.0 / math.pi)
GELU_C1 = GELU_C0 * 0.044715


def _gelu_tanh(x):
    hx = 0.5 * x
    return hx + hx * jnp.tanh(x * (GELU_C0 + GELU_C1 * (x * x)))


def _ffn_kernel(x_ref, gpre_ref, wup_ref, cw_ref, cb_ref, wd_ref, gpost_ref, o_ref,
                ubuf_ref, f_ref, carry_ref, *, tm, tiles_per_seq):
    first = (pl.program_id(0) % tiles_per_seq) == 0
    fc = FFN_COLS
    hn = _rms(x_ref[...], gpre_ref[...]).astype(BF16)
    for j in range(D_FF // fc):
        halves = []
        for half in range(2):
            cols = slice(half * D_FF + j * fc, half * D_FF + (j + 1) * fc)
            buf = ubuf_ref.at[2 * (j % 2) + half]
            u = _dot(hn, wup_ref[:, cols])
            buf[0:SUBLANES, :] = jnp.where(first, 0.0, carry_ref[:, cols])
            buf[SUBLANES:, :] = u
            carry_ref[:, cols] = u[tm - SUBLANES:, :]
            out = cb_ref[:, cols] + cw_ref[0:1, cols] * buf[SUBLANES - 2:SUBLANES - 2 + tm, :]
            out = out + cw_ref[1:2, cols] * buf[SUBLANES - 1:SUBLANES - 1 + tm, :]
            halves.append(out + cw_ref[2:3, cols] * u)
        f_ref[:, j * fc:(j + 1) * fc] = (_gelu_tanh(halves[0]) * halves[1]).astype(BF16)
    acc = _dot(f_ref[...], wd_ref[...])
    o_ref[...] = x_ref[...] + _rms(acc, gpost_ref[...])


def _ffn(x, g_pre, w_up, conv_w, conv_b, w_down, g_post, seq, tm):
    t, d = x.shape
    xblk = pl.BlockSpec((tm, d), lambda i: (i, 0))
    const = lambda *shape: pl.BlockSpec(shape, lambda i: (0,) * len(shape))
    return pl.pallas_call(
        functools.partial(_ffn_kernel, tm=tm, tiles_per_seq=seq // tm),
        grid=(t // tm,),
        in_specs=[xblk, const(1, d), const(d, 2 * D_FF), const(CONV_WIDTH, 2 * D_FF),
                  const(1, 2 * D_FF), const(D_FF, d), const(1, d)],
        out_specs=xblk,
        out_shape=jax.ShapeDtypeStruct((t, d), F32),
        scratch_shapes=[pltpu.VMEM((4, SUBLANES + tm, FFN_COLS), F32),
                        pltpu.VMEM((tm, D_FF), BF16),
                        pltpu.VMEM((SUBLANES, 2 * D_FF), F32)],
        compiler_params=_params("arbitrary"),
        name="conv_ffn",
    )(x, g_pre, w_up, conv_w, conv_b, w_down, g_post)


def _pad_rows(w, before, total):
    return jnp.pad(w, ((before, total - before - w.shape[0]), (0, 0)))


def kernel(x, mem, mem_norm, norm_mix_pre, norm_mix_post, w_in, mu_shift, pool_w, pool_b,
           pool_scale, w_proj_pool, w_mem_kv, w_proj_mem, w0, w_up_decay, a0, w_up_a, w_up_g,
           k_k, k_a, r_k, ln_x_w, ln_x_b, v0, w_down_v, w_up_v, w_proj_rwkv, gate_b, w_o,
           norm_ffn_pre, norm_ffn_post, w_ffn_up, conv_w, conv_b, w_ffn_down):
    batch, seq, d = x.shape
    depth = w_in.shape[0]
    assert d == D_MODEL and seq % 512 == 0 and mem.shape[1] == MEM_LEN
    t = batch * seq
    xf = x.reshape(t, d)
    memf = mem.reshape(batch * MEM_LEN, d)
    row = lambda a: a.reshape(1, -1)
    bf = lambda a: a.astype(BF16)

    v_first = None
    for l in range(depth):
        w_pqg = jnp.concatenate([w_in[l][:, :OFF_RWKV], w_in[l][:, OFF_GATE:]], axis=1)
        z = _norm_matmul(xf, row(norm_mix_pre[l]), bf(w_pqg), 512, w_pqg.shape[1] // 2)
        kv = _norm_matmul(memf, row(mem_norm), bf(w_mem_kv[l]), 512, 2 * MEM_WIDTH)
        mp = _branches(z, kv, bf(pool_w[l]), row(pool_b[l]), row(pool_scale[l]), bf(w_proj_pool[l]),
                       bf(w_proj_mem[l]), gate_b[l], batch, seq, 512)

        vres = None
        if l > 0:
            vres = (v_first, row(v0[l - 1]),
                    bf(jnp.pad(w_down_v[l - 1], ((0, 0), (0, LANES - VRES_LORA)))),
                    bf(_pad_rows(w_up_v[l - 1], 0, LANES)))
        r, lw, k, v, kk, b, g = _rwkv_in(
            xf, row(norm_mix_pre[l]), bf(w_in[l][:, OFF_RWKV:OFF_GATE]), row(mu_shift[l]), row(w0[l]),
            bf(_pad_rows(w_up_decay[l], 0, LANES)), row(a0[l]),
            bf(_pad_rows(w_up_a[l], DECAY_LORA, LANES)), bf(w_up_g[l]), row(k_k[l]), row(k_a[l]),
            vres, seq, 256)
        if l == 0:
            v_first = v
        yg = _scan(r, lw, k, v, kk, b, g, row(r_k[l]), row(ln_x_w[l]), row(ln_x_b[l]), batch, seq)
        xf = _merge(yg, mp, z, xf, bf(w_proj_rwkv[l]), bf(w_o[l]), gate_b[l], row(norm_mix_post[l]), 512)
        xf = _ffn(xf, row(norm_ffn_pre[l]), bf(w_ffn_up[l]), conv_w[l], row(conv_b[l]),
                  bf(w_ffn_down[l]), row(norm_ffn_post[l]), seq, 512)
    return xf.reshape(batch, seq, d)
```

```python
import functools
import math

import jax
import jax.numpy as jnp
from jax import lax
from jax.experimental import pallas as pl
from jax.experimental.pallas import tpu as pltpu

F32 = jnp.float32
BF16 = jnp.bfloat16

D_MODEL = 1024
MEM_LEN = 256
POOL_WIDTH = 512
POOL_GROUP_DIM = 128
POOL_WINDOWS = (2, 4, 8, 16)
POOL_HALO = 16
MEM_HEADS = 4
MEM_HEAD_DIM = 128
MEM_WIDTH = MEM_HEADS * MEM_HEAD_DIM
RWKV_HEAD_DIM = 64
RWKV_WIDTH = D_MODEL
DECAY_LORA = 64
ICLR_LORA = 64
VRES_LORA = 32
GATE_LORA = 128
LORA_COLS = DECAY_LORA + ICLR_LORA + GATE_LORA
RWKV_COLS = 3 * RWKV_WIDTH + LORA_COLS
N_BRANCH = 3
OFF_Q = POOL_WIDTH
OFF_RWKV = OFF_Q + MEM_WIDTH
OFF_GATE = OFF_RWKV + RWKV_COLS
IN_COLS = OFF_GATE + N_BRANCH * D_MODEL
D_FF = 2816
CONV_WIDTH = 3
NORM_EPS = 1e-6
LNX_EPS = 64e-5
L2_EPS = 1e-12

LANES = 128
SUBLANES = 8
VMEM_LIMIT_BYTES = 56 * 1024 * 1024

SCAN_CHUNK = 64
SCAN_SEQS = 2
HEADS_PER_SLAB = LANES // RWKV_HEAD_DIM
N_SLABS = RWKV_WIDTH // LANES
FFN_COLS = 256

ZBLK_PQ, ZBLK_G0, ZBLK_G1, ZBLK_G2 = range(4)


def _params(*sem):
    return pltpu.CompilerParams(dimension_semantics=sem, vmem_limit_bytes=VMEM_LIMIT_BYTES)


def _dot(a, b):
    return jnp.dot(a, b, preferred_element_type=F32)


def _dot_nt(a, b):
    return lax.dot_general(a, b, (((1,), (1,)), ((), ())), preferred_element_type=F32)


def _dot_tn(a, b):
    return lax.dot_general(a, b, (((0,), (0,)), ((), ())), preferred_element_type=F32)


def _sigmoid(x):
    return 1.0 / (1.0 + jnp.exp(-x))


def _rms(x, g):
    ms = jnp.mean(x * x, axis=-1, keepdims=True)
    return x * lax.rsqrt(ms + NORM_EPS) * g


def _split2(x):
    hi = x.astype(BF16)
    lo = (x - hi.astype(F32)).astype(BF16)
    return hi, lo


def _head_sums(xs, seg):
    rows = xs[0].shape[0]
    parts = [p for x in xs for p in _split2(x)]
    res = _dot(jnp.concatenate(parts, axis=0), seg)
    return [res[2 * i * rows:(2 * i + 1) * rows] + res[(2 * i + 1) * rows:(2 * i + 2) * rows]
            for i in range(len(xs))]


def _seg_matrix():
    i = lax.broadcasted_iota(jnp.int32, (LANES, LANES), 0) // RWKV_HEAD_DIM
    j = lax.broadcasted_iota(jnp.int32, (LANES, LANES), 1) // RWKV_HEAD_DIM
    return jnp.where(i == j, 1.0, 0.0).astype(BF16)


def _norm_matmul_kernel(x_ref, g_ref, w_ref, o_ref):
    o_ref[...] = _dot(_rms(x_ref[...], g_ref[...]).astype(BF16), w_ref[...])


def _norm_matmul(x, g, w, tm, tn):
    t, d = x.shape
    n = w.shape[1]
    return pl.pallas_call(
        _norm_matmul_kernel,
        grid=(n // tn, t // tm),
        in_specs=[pl.BlockSpec((tm, d), lambda j, i: (i, 0)),
                  pl.BlockSpec((1, d), lambda j, i: (0, 0)),
                  pl.BlockSpec((d, tn), lambda j, i: (0, j))],
        out_specs=pl.BlockSpec((tm, tn), lambda j, i: (i, j)),
        out_shape=jax.ShapeDtypeStruct((t, n), F32),
        compiler_params=_params("parallel", "parallel"),
        name="norm_matmul",
    )(x, g, w)


def _mix_out_kernel(zpq_ref, halo_ref, zg0_ref, zg1_ref, zg2_ref, kv_ref, yg_ref, x_ref,
                    poolw_ref, poolb_ref, pools_ref, wpp_ref, wpm_ref, wpr_ref, wo_ref, gb_ref,
                    gn_ref, o_ref, *, tm):
    i = pl.program_id(1)
    hd = [slice(h * MEM_HEAD_DIM, (h + 1) * MEM_HEAD_DIM) for h in range(MEM_HEADS)]
    scores = [_dot_nt(zpq_ref[:, POOL_WIDTH + sl.start:POOL_WIDTH + sl.stop].astype(BF16),
                      kv_ref[:, sl].astype(BF16)) * (MEM_HEAD_DIM ** -0.5) for sl in hd]
    y_rwkv = _dot(yg_ref[...], wpr_ref[...])

    p = zpq_ref[:, :POOL_WIDTH]
    prev = jnp.where(i == 0, 0.0, halo_ref[...])
    pe = jnp.concatenate([prev, p], axis=0)
    pos = lax.broadcasted_iota(jnp.int32, (tm, POOL_GROUP_DIM), 0) + i * tm
    mixed = []
    for gi, win in enumerate(POOL_WINDOWS):
        sl = slice(gi * POOL_GROUP_DIM, (gi + 1) * POOL_GROUP_DIM)
        s = pe[:, sl]
        k = 1
        while k < win:
            s = s + pltpu.roll(s, k, axis=0)
            k *= 2
        cnt = jnp.minimum(pos + 1, win).astype(F32)
        pooled = s[POOL_HALO:] / cnt - p[:, sl]
        m = _dot(pooled.astype(BF16), poolw_ref[gi]) + poolb_ref[:, sl]
        mixed.append((m * pools_ref[:, sl]).astype(BF16))
    y_pool = _dot(jnp.concatenate(mixed, axis=1), wpp_ref[...])

    probs = []
    for s in scores:
        e = jnp.exp(s - jnp.max(s, axis=-1, keepdims=True))
        probs.append((e / jnp.sum(e, axis=-1, keepdims=True)).astype(BF16))
    heads = [_dot(probs[h], kv_ref[:, MEM_WIDTH + sl.start:MEM_WIDTH + sl.stop].astype(BF16)).astype(BF16)
             for h, sl in enumerate(hd)]
    y_mem = _dot(jnp.concatenate(heads, axis=1), wpm_ref[...])

    gate = lambda z_ref, n: _sigmoid(z_ref[...] + gb_ref[n:n + 1, :])
    merged = gate(zg0_ref, 0) * y_pool + gate(zg1_ref, 1) * y_rwkv + gate(zg2_ref, 2) * y_mem
    o = _dot(merged.astype(BF16), wo_ref[...])
    o_ref[...] = x_ref[...] + _rms(o, gn_ref[...])


def _mix_out(z, kv, yg, x, pool_w, pool_b, pool_scale, w_proj_pool, w_proj_mem, w_proj_rwkv, w_o,
             gate_b, g_post, batch, seq, tm):
    t, d = x.shape
    nt = seq // tm
    hb = tm // POOL_HALO
    const = lambda *shape: pl.BlockSpec(shape, lambda b, i: (0,) * len(shape))
    zblk = lambda blk: pl.BlockSpec((tm, d), lambda b, i: (b * nt + i, blk))
    return pl.pallas_call(
        functools.partial(_mix_out_kernel, tm=tm),
        grid=(batch, nt),
        in_specs=[zblk(ZBLK_PQ),
                  pl.BlockSpec((POOL_HALO, POOL_WIDTH),
                               lambda b, i: (jnp.maximum((b * nt + i) * hb - 1, 0), 0)),
                  zblk(ZBLK_G0), zblk(ZBLK_G1), zblk(ZBLK_G2),
                  pl.BlockSpec((MEM_LEN, 2 * MEM_WIDTH), lambda b, i: (b, 0)),
                  zblk(0), zblk(0),
                  const(len(POOL_WINDOWS), POOL_GROUP_DIM, POOL_GROUP_DIM),
                  const(1, POOL_WIDTH), const(1, POOL_WIDTH),
                  const(POOL_WIDTH, d), const(MEM_WIDTH, d), const(d, d), const(d, d),
                  const(N_BRANCH, d), const(1, d)],
        out_specs=zblk(0),
        out_shape=jax.ShapeDtypeStruct((t, d), F32),
        compiler_params=_params("parallel", "arbitrary"),
        name="mix_out",
    )(z, z, z, z, z, kv, yg, x, pool_w, pool_b, pool_scale, w_proj_pool, w_proj_mem, w_proj_rwkv,
      w_o, gate_b, g_post)


def _rwkv_in_kernel(*refs, tm, tiles_per_seq, has_vres):
    (x_ref, gn_ref, w_ref, mu_ref, w0_ref, wdec_ref, a0_ref, wa_ref, wg_ref, kk_ref, ka_ref) = refs[:11]
    refs = refs[11:]
    if has_vres:
        vfirst_ref, v0_ref, wdv_ref, wuv_ref = refs[:4]
        refs = refs[4:]
    r_o, lw_o, k_o, v_o, kk_o, b_o, g_o, carry_ref = refs
    d = RWKV_WIDTH

    xn = _rms(x_ref[...], gn_ref[...]).astype(BF16)
    first = (pl.program_id(0) % tiles_per_seq) == 0

    def project(lo, hi):
        z = _dot(xn, w_ref[:, lo:hi])
        last = jnp.where(first, 0.0, carry_ref[SUBLANES - 1:SUBLANES, lo:hi])
        carry_ref[:, lo:hi] = z[tm - SUBLANES:, :]
        row = lax.broadcasted_iota(jnp.int32, z.shape, 0)
        zp = jnp.where(row == 0, last, pltpu.roll(z, 1, axis=0))
        return z + (zp - z) * mu_ref[:, lo:hi]

    lora = project(3 * d, RWKV_COLS)
    dwa = lora[:, :LANES]
    dg = lora[:, LANES:]
    r_o[...] = project(0, d)
    k = project(d, 2 * d)

    u = w0_ref[...] + _dot(jnp.tanh(dwa).astype(BF16), wdec_ref[...])
    lw_o[...] = -math.exp(-0.5) * _sigmoid(u)
    alpha = _sigmoid(a0_ref[...] + _dot(dwa.astype(BF16), wa_ref[...]))
    g_o[...] = _dot(_sigmoid(dg).astype(BF16), wg_ref[...])

    v = project(2 * d, 3 * d)

    seg = _seg_matrix()
    kk = k * kk_ref[...]
    kk2 = kk * kk
    ss = jnp.concatenate(_head_sums([kk2[:, s * LANES:(s + 1) * LANES] for s in range(N_SLABS)], seg), axis=1)
    kk = kk * lax.rsqrt(ss + L2_EPS)
    k_o[...] = k * (1.0 + (alpha - 1.0) * ka_ref[...])
    kk_o[...] = kk
    b_o[...] = kk * alpha

    if has_vres:
        low = _dot(v.astype(BF16), wdv_ref[...])
        vg = _sigmoid(v0_ref[...] + _dot(low.astype(BF16), wuv_ref[...]))
        v = v + (vfirst_ref[...] - v) * vg
    v_o[...] = v


def _rwkv_in(x, g_pre, w, mu, w0, wdec, a0, wa, wg, k_k, k_a, vres, seq, tm):
    t, d = x.shape
    const = lambda *shape: pl.BlockSpec(shape, lambda i: (0,) * len(shape))
    blk = pl.BlockSpec((tm, d), lambda i: (i, 0))
    in_specs = [blk, const(1, d), const(d, RWKV_COLS), const(1, RWKV_COLS), const(1, d),
                const(LANES, d), const(1, d), const(LANES, d), const(GATE_LORA, d), const(1, d),
                const(1, d)]
    args = [x, g_pre, w, mu, w0, wdec, a0, wa, wg, k_k, k_a]
    if vres is not None:
        in_specs += [blk, const(1, d), const(d, LANES), const(LANES, d)]
        args += list(vres)
    return pl.pallas_call(
        functools.partial(_rwkv_in_kernel, tm=tm, tiles_per_seq=seq // tm, has_vres=vres is not None),
        grid=(t // tm,),
        in_specs=in_specs,
        out_specs=[blk] * 7,
        out_shape=[jax.ShapeDtypeStruct((t, d), F32)] * 7,
        scratch_shapes=[pltpu.VMEM((SUBLANES, RWKV_COLS), F32)],
        compiler_params=_params("arbitrary"),
        name="rwkv_in",
    )(*args)


def _stack_heads(x, m1):
    return jnp.concatenate([jnp.where(m1, x, 0.0), jnp.where(m1, 0.0, x)], axis=0)


def _scan_kernel(r_ref, lw_ref, k_ref, v_ref, kk_ref, b_ref, g_ref, rk_ref, lnw_ref, lnb_ref,
                 o_ref, h_ref):
    c = SCAN_CHUNK
    c2 = HEADS_PER_SLAB * c

    @pl.when(pl.program_id(1) == 0)
    def _():
        h_ref[...] = jnp.zeros_like(h_ref)

    lane = lax.broadcasted_iota(jnp.int32, (c, LANES), 1)
    m1 = lane < RWKV_HEAD_DIM
    ti = lax.broadcasted_iota(jnp.int32, (c, LANES), 0)
    tj = lane % RWKV_HEAD_DIM
    strict = ti > tj
    incl = ti >= tj
    eye = jnp.where(ti == tj, 1.0, 0.0)
    seg = _seg_matrix()
    inv_n = 1.0 / RWKV_HEAD_DIM
    ltri = jnp.where(lax.broadcasted_iota(jnp.int32, (c, c), 0)
                     >= lax.broadcasted_iota(jnp.int32, (c, c), 1), 1.0, 0.0).astype(BF16)

    units = [(j, slice(s * LANES, (s + 1) * LANES)) for j in range(SCAN_SEQS) for s in range(N_SLABS)]
    nu = range(len(units))

    lws, cums = [], []
    for j in range(SCAN_SEQS):
        lw = lw_ref[j]
        hi = lw.astype(BF16)
        r1 = lw - hi.astype(F32)
        mid = r1.astype(BF16)
        lo = (r1 - mid.astype(F32)).astype(BF16)
        lws.append(lw)
        cums.append(_dot(ltri, hi) + _dot(ltri, mid) + _dot(ltri, lo))

    ar, d_ak, d_r, tinv, apow, bk, decay_c = [], [], [], [], [], [], []
    for j, sl in units:
        cum_s = cums[j][:, sl]
        last = cum_s[c - 1:c, :]
        p_inv = jnp.exp(-cum_s)
        p_end = jnp.exp(last - cum_s)
        kx = k_ref[j, :, sl]
        bb = b_ref[j, :, sl]
        a_t = -kk_ref[j, :, sl] * jnp.exp(cum_s - lws[j][:, sl])
        r_t = r_ref[j, :, sl] * jnp.exp(cum_s)
        ar_s = jnp.concatenate([a_t, r_t], axis=0).astype(BF16)
        xr = jnp.concatenate([_stack_heads(bb * p_inv, m1), _stack_heads(kx * p_inv, m1)],
                             axis=0).astype(BF16)
        gm = _dot_nt(ar_s, xr)
        d_ab = jnp.where(strict, gm[:c, :c2], 0.0)
        d_ak.append(jnp.where(strict, gm[:c, c2:], 0.0).astype(BF16))
        d_r.append(jnp.concatenate([jnp.where(incl, gm[c:, :c2], 0.0),
                                    jnp.where(incl, gm[c:, c2:], 0.0)], axis=1).astype(BF16))
        tinv.append(eye + d_ab)
        apow.append(d_ab.astype(BF16))
        ar.append(ar_s)
        bk.append(jnp.concatenate([_stack_heads(bb * p_end, m1), _stack_heads(kx * p_end, m1)],
                                  axis=0).astype(BF16))
        decay_c.append(jnp.exp(last))

    for i in nu:
        apow[i] = _dot(apow[i], _stack_heads(apow[i], m1)).astype(BF16)
    n = 4
    while n < c:
        for i in nu:
            both = _dot(jnp.concatenate([tinv[i].astype(BF16), apow[i]], axis=0),
                        _stack_heads(apow[i], m1))
            tinv[i] = tinv[i] + both[:c]
            apow[i] = both[c:].astype(BF16)
        n *= 2
    for i in nu:
        tinv[i] = (tinv[i] + _dot(tinv[i].astype(BF16), _stack_heads(apow[i], m1))).astype(BF16)

    hs = [h_ref[i] for i in nu]
    xh = [_dot_nt(ar[i], hs[i].astype(BF16)) for i in nu]
    v_st = [_stack_heads(v_ref[j, :, sl], m1).astype(BF16) for j, sl in units]
    z = [xh[i][:c] + _dot(d_ak[i], v_st[i]) for i in nu]
    u = [_dot(tinv[i], _stack_heads(z[i], m1).astype(BF16)) for i in nu]
    uv = [jnp.concatenate([_stack_heads(u[i], m1).astype(BF16), v_st[i]], axis=0) for i in nu]
    y = [xh[i][c:] + _dot(d_r[i], uv[i]) for i in nu]
    for i in nu:
        h_ref[i] = hs[i] * decay_c[i] + _dot_tn(uv[i], bk[i])

    bonus = _head_sums([r_ref[j, :, sl] * k_ref[j, :, sl] * rk_ref[:, sl] for j, sl in units], seg)
    mu = _head_sums(y, seg)
    dlt = [y[i] - mu[i] * inv_n for i in nu]
    var = _head_sums([dlt[i] * dlt[i] for i in nu], seg)
    for i, (j, sl) in enumerate(units):
        yn = dlt[i] * lax.rsqrt(var[i] * inv_n + LNX_EPS) * lnw_ref[:, sl] + lnb_ref[:, sl]
        o_ref[j, :, sl] = ((yn + bonus[i] * v_ref[j, :, sl]) * g_ref[j, :, sl]).astype(BF16)


def _scan(r, lw, k, v, kk, b, g, r_k, ln_w, ln_b, batch, seq):
    t, d = r.shape
    c = SCAN_CHUNK
    as_seqs = lambda a: a.reshape(batch, seq, d)
    blk = pl.BlockSpec((SCAN_SEQS, c, d), lambda bi, ci: (bi, ci, 0))
    row = pl.BlockSpec((1, d), lambda bi, ci: (0, 0))
    out = pl.pallas_call(
        _scan_kernel,
        grid=(batch // SCAN_SEQS, seq // c),
        in_specs=[blk] * 7 + [row] * 3,
        out_specs=blk,
        out_shape=jax.ShapeDtypeStruct((batch, seq, d), BF16),
        scratch_shapes=[pltpu.VMEM((SCAN_SEQS * N_SLABS, LANES, LANES), F32)],
        compiler_params=_params("parallel", "arbitrary"),
        name="rwkv_scan",
    )(*(as_seqs(a) for a in (r, lw, k, v, kk, b, g)), r_k, ln_w, ln_b)
    return out.reshape(t, d)


GELU_C0 = math.sqrt(2.0 / math.pi)
GELU_C1 = GELU_C0 * 0.044715


def _gelu_tanh(x):
    hx = 0.5 * x
    return hx + hx * jnp.tanh(x * (GELU_C0 + GELU_C1 * (x * x)))


def _ffn_kernel(x_ref, gpre_ref, wup_ref, cw_ref, cb_ref, wd_ref, gpost_ref, o_ref,
                ubuf_ref, f_ref, carry_ref, *, tm, tiles_per_seq):
    first = (pl.program_id(0) % tiles_per_seq) == 0
    fc = FFN_COLS
    hn = _rms(x_ref[...], gpre_ref[...]).astype(BF16)
    for j in range(D_FF // fc):
        halves = []
        for half in range(2):
            cols = slice(half * D_FF + j * fc, half * D_FF + (j + 1) * fc)
            buf = ubuf_ref.at[2 * (j % 2) + half]
            u = _dot(hn, wup_ref[:, cols])
            buf[0:SUBLANES, :] = jnp.where(first, 0.0, carry_ref[:, cols])
            buf[SUBLANES:, :] = u
            carry_ref[:, cols] = u[tm - SUBLANES:, :]
            out = cb_ref[:, cols] + cw_ref[0:1, cols] * buf[SUBLANES - 2:SUBLANES - 2 + tm, :]
            out = out + cw_ref[1:2, cols] * buf[SUBLANES - 1:SUBLANES - 1 + tm, :]
            halves.append(out + cw_ref[2:3, cols] * u)
        f_ref[:, j * fc:(j + 1) * fc] = (_gelu_tanh(halves[0]) * halves[1]).astype(BF16)
    acc = _dot(f_ref[...], wd_ref[...])
    o_ref[...] = x_ref[...] + _rms(acc, gpost_ref[...])


def _ffn(x, g_pre, w_up, conv_w, conv_b, w_down, g_post, seq, tm):
    t, d = x.shape
    xblk = pl.BlockSpec((tm, d), lambda i: (i, 0))
    const = lambda *shape: pl.BlockSpec(shape, lambda i: (0,) * len(shape))
    return pl.pallas_call(
        functools.partial(_ffn_kernel, tm=tm, tiles_per_seq=seq // tm),
        grid=(t // tm,),
        in_specs=[xblk, const(1, d), const(d, 2 * D_FF), const(CONV_WIDTH, 2 * D_FF),
                  const(1, 2 * D_FF), const(D_FF, d), const(1, d)],
        out_specs=xblk,
        out_shape=jax.ShapeDtypeStruct((t, d), F32),
        scratch_shapes=[pltpu.VMEM((4, SUBLANES + tm, FFN_COLS), F32),
                        pltpu.VMEM((tm, D_FF), BF16),
                        pltpu.VMEM((SUBLANES, 2 * D_FF), F32)],
        compiler_params=_params("arbitrary"),
        name="conv_ffn",
    )(x, g_pre, w_up, conv_w, conv_b, w_down, g_post)


def _pad_rows(w, before, total):
    return jnp.pad(w, ((before, total - before - w.shape[0]), (0, 0)))


def kernel(x, mem, mem_norm, norm_mix_pre, norm_mix_post, w_in, mu_shift, pool_w, pool_b,
           pool_scale, w_proj_pool, w_mem_kv, w_proj_mem, w0, w_up_decay, a0, w_up_a, w_up_g,
           k_k, k_a, r_k, ln_x_w, ln_x_b, v0, w_down_v, w_up_v, w_proj_rwkv, gate_b, w_o,
           norm_ffn_pre, norm_ffn_post, w_ffn_up, conv_w, conv_b, w_ffn_down):
    batch, seq, d = x.shape
    depth = w_in.shape[0]
    assert d == D_MODEL and seq % 512 == 0 and mem.shape[1] == MEM_LEN
    t = batch * seq
    xf = x.reshape(t, d)
    memf = mem.reshape(batch * MEM_LEN, d)
    row = lambda a: a.reshape(1, -1)
    bf = lambda a: a.astype(BF16)

    v_first = None
    for l in range(depth):
        w_pqg = jnp.concatenate([w_in[l][:, :OFF_RWKV], w_in[l][:, OFF_GATE:]], axis=1)
        z = _norm_matmul(xf, row(norm_mix_pre[l]), bf(w_pqg), 512, w_pqg.shape[1] // 2)
        kv = _norm_matmul(memf, row(mem_norm), bf(w_mem_kv[l]), 512, 2 * MEM_WIDTH)
        vres = None
        if l > 0:
            vres = (v_first, row(v0[l - 1]),
                    bf(jnp.pad(w_down_v[l - 1], ((0, 0), (0, LANES - VRES_LORA)))),
                    bf(_pad_rows(w_up_v[l - 1], 0, LANES)))
        r, lw, k, v, kk, b, g = _rwkv_in(
            xf, row(norm_mix_pre[l]), bf(w_in[l][:, OFF_RWKV:OFF_GATE]), row(mu_shift[l]), row(w0[l]),
            bf(_pad_rows(w_up_decay[l], 0, LANES)), row(a0[l]),
            bf(_pad_rows(w_up_a[l], DECAY_LORA, LANES)), bf(w_up_g[l]), row(k_k[l]), row(k_a[l]),
            vres, seq, 256)
        if l == 0:
            v_first = v
        yg = _scan(r, lw, k, v, kk, b, g, row(r_k[l]), row(ln_x_w[l]), row(ln_x_b[l]), batch, seq)
        xf = _mix_out(z, kv, yg, xf, bf(pool_w[l]), row(pool_b[l]), row(pool_scale[l]),
                      bf(w_proj_pool[l]), bf(w_proj_mem[l]), bf(w_proj_rwkv[l]), bf(w_o[l]), gate_b[l],
                      row(norm_mix_post[l]), batch, seq, 512)
        xf = _ffn(xf, row(norm_ffn_pre[l]), bf(w_ffn_up[l]), conv_w[l], row(conv_b[l]),
                  bf(w_ffn_down[l]), row(norm_ffn_post[l]), seq, 512)
    return xf.reshape(batch, seq, d)
```

```python
import functools
import math

import jax
import jax.numpy as jnp
from jax import lax
from jax.experimental import pallas as pl
from jax.experimental.pallas import tpu as pltpu

F32 = jnp.float32
BF16 = jnp.bfloat16

D_MODEL = 1024
MEM_LEN = 256
POOL_WIDTH = 512
POOL_GROUP_DIM = 128
POOL_WINDOWS = (2, 4, 8, 16)
POOL_HALO = 16
MEM_HEADS = 4
MEM_HEAD_DIM = 128
MEM_WIDTH = MEM_HEADS * MEM_HEAD_DIM
RWKV_HEAD_DIM = 64
RWKV_WIDTH = D_MODEL
DECAY_LORA = 64
ICLR_LORA = 64
VRES_LORA = 32
GATE_LORA = 128
LORA_COLS = DECAY_LORA + ICLR_LORA + GATE_LORA
RWKV_COLS = 3 * RWKV_WIDTH + LORA_COLS
N_BRANCH = 3
OFF_Q = POOL_WIDTH
OFF_RWKV = OFF_Q + MEM_WIDTH
OFF_GATE = OFF_RWKV + RWKV_COLS
IN_COLS = OFF_GATE + N_BRANCH * D_MODEL
D_FF = 2816
CONV_WIDTH = 3
NORM_EPS = 1e-6
LNX_EPS = 64e-5
L2_EPS = 1e-12

LANES = 128
SUBLANES = 8
VMEM_LIMIT_BYTES = 56 * 1024 * 1024

SCAN_CHUNK = 64
SCAN_SEQS = 4
HEADS_PER_SLAB = LANES // RWKV_HEAD_DIM
N_SLABS = RWKV_WIDTH // LANES
FFN_COLS = 256

ZBLK_PQ, ZBLK_G0, ZBLK_G1, ZBLK_G2 = range(4)


def _params(*sem):
    return pltpu.CompilerParams(dimension_semantics=sem, vmem_limit_bytes=VMEM_LIMIT_BYTES)


def _dot(a, b):
    return jnp.dot(a, b, preferred_element_type=F32)


def _dot_nt(a, b):
    return lax.dot_general(a, b, (((1,), (1,)), ((), ())), preferred_element_type=F32)


def _dot_tn(a, b):
    return lax.dot_general(a, b, (((0,), (0,)), ((), ())), preferred_element_type=F32)


def _sigmoid(x):
    return 1.0 / (1.0 + jnp.exp(-x))


def _rms(x, g):
    ms = jnp.mean(x * x, axis=-1, keepdims=True)
    return x * lax.rsqrt(ms + NORM_EPS) * g


def _split2(x):
    hi = x.astype(BF16)
    lo = (x - hi.astype(F32)).astype(BF16)
    return hi, lo


def _head_sums(xs, seg):
    rows = xs[0].shape[0]
    parts = [p for x in xs for p in _split2(x)]
    res = _dot(jnp.concatenate(parts, axis=0), seg)
    return [res[2 * i * rows:(2 * i + 1) * rows] + res[(2 * i + 1) * rows:(2 * i + 2) * rows]
            for i in range(len(xs))]


def _seg_matrix():
    i = lax.broadcasted_iota(jnp.int32, (LANES, LANES), 0) // RWKV_HEAD_DIM
    j = lax.broadcasted_iota(jnp.int32, (LANES, LANES), 1) // RWKV_HEAD_DIM
    return jnp.where(i == j, 1.0, 0.0).astype(BF16)


def _norm_matmul_kernel(x_ref, g_ref, w_ref, o_ref):
    o_ref[...] = _dot(_rms(x_ref[...], g_ref[...]).astype(BF16), w_ref[...]).astype(o_ref.dtype)


def _norm_matmul(x, g, w, tm, tn):
    t, d = x.shape
    n = w.shape[1]
    return pl.pallas_call(
        _norm_matmul_kernel,
        grid=(n // tn, t // tm),
        in_specs=[pl.BlockSpec((tm, d), lambda j, i: (i, 0)),
                  pl.BlockSpec((1, d), lambda j, i: (0, 0)),
                  pl.BlockSpec((d, tn), lambda j, i: (0, j))],
        out_specs=pl.BlockSpec((tm, tn), lambda j, i: (i, j)),
        out_shape=jax.ShapeDtypeStruct((t, n), BF16),
        compiler_params=_params("parallel", "parallel"),
        name="norm_matmul",
    )(x, g, w)


def _mix_out_kernel(zpq_ref, halo_ref, zg0_ref, zg1_ref, zg2_ref, kv_ref, yg_ref, x_ref,
                    poolw_ref, poolb_ref, pools_ref, wpp_ref, wpm_ref, wpr_ref, wo_ref, gb_ref,
                    gn_ref, o_ref, *, tm):
    i = pl.program_id(1)
    hd = [slice(h * MEM_HEAD_DIM, (h + 1) * MEM_HEAD_DIM) for h in range(MEM_HEADS)]
    scores = [_dot_nt(zpq_ref[:, POOL_WIDTH + sl.start:POOL_WIDTH + sl.stop].astype(BF16),
                      kv_ref[:, sl].astype(BF16)) * (MEM_HEAD_DIM ** -0.5) for sl in hd]
    y_rwkv = _dot(yg_ref[...], wpr_ref[...])

    p = zpq_ref[:, :POOL_WIDTH].astype(F32)
    prev = jnp.where(i == 0, 0.0, halo_ref[...].astype(F32))
    pe = jnp.concatenate([prev, p], axis=0)
    pos = lax.broadcasted_iota(jnp.int32, (tm, POOL_GROUP_DIM), 0) + i * tm
    mixed = []
    for gi, win in enumerate(POOL_WINDOWS):
        sl = slice(gi * POOL_GROUP_DIM, (gi + 1) * POOL_GROUP_DIM)
        s = pe[:, sl]
        k = 1
        while k < win:
            s = s + pltpu.roll(s, k, axis=0)
            k *= 2
        cnt = jnp.minimum(pos + 1, win).astype(F32)
        pooled = s[POOL_HALO:] / cnt - p[:, sl]
        m = _dot(pooled.astype(BF16), poolw_ref[gi]) + poolb_ref[:, sl]
        mixed.append((m * pools_ref[:, sl]).astype(BF16))
    y_pool = _dot(jnp.concatenate(mixed, axis=1), wpp_ref[...])

    probs = []
    for s in scores:
        e = jnp.exp(s - jnp.max(s, axis=-1, keepdims=True))
        probs.append((e / jnp.sum(e, axis=-1, keepdims=True)).astype(BF16))
    heads = [_dot(probs[h], kv_ref[:, MEM_WIDTH + sl.start:MEM_WIDTH + sl.stop].astype(BF16)).astype(BF16)
             for h, sl in enumerate(hd)]
    y_mem = _dot(jnp.concatenate(heads, axis=1), wpm_ref[...])

    gate = lambda z_ref, n: _sigmoid(z_ref[...].astype(F32) + gb_ref[n:n + 1, :])
    merged = gate(zg0_ref, 0) * y_pool + gate(zg1_ref, 1) * y_rwkv + gate(zg2_ref, 2) * y_mem
    o = _dot(merged.astype(BF16), wo_ref[...])
    o_ref[...] = x_ref[...] + _rms(o, gn_ref[...])


def _mix_out(z, kv, yg, x, pool_w, pool_b, pool_scale, w_proj_pool, w_proj_mem, w_proj_rwkv, w_o,
             gate_b, g_post, batch, seq, tm):
    t, d = x.shape
    nt = seq // tm
    hb = tm // POOL_HALO
    const = lambda *shape: pl.BlockSpec(shape, lambda b, i: (0,) * len(shape))
    zblk = lambda blk: pl.BlockSpec((tm, d), lambda b, i: (b * nt + i, blk))
    return pl.pallas_call(
        functools.partial(_mix_out_kernel, tm=tm),
        grid=(batch, nt),
        in_specs=[zblk(ZBLK_PQ),
                  pl.BlockSpec((POOL_HALO, POOL_WIDTH),
                               lambda b, i: (jnp.maximum((b * nt + i) * hb - 1, 0), 0)),
                  zblk(ZBLK_G0), zblk(ZBLK_G1), zblk(ZBLK_G2),
                  pl.BlockSpec((MEM_LEN, 2 * MEM_WIDTH), lambda b, i: (b, 0)),
                  zblk(0), zblk(0),
                  const(len(POOL_WINDOWS), POOL_GROUP_DIM, POOL_GROUP_DIM),
                  const(1, POOL_WIDTH), const(1, POOL_WIDTH),
                  const(POOL_WIDTH, d), const(MEM_WIDTH, d), const(d, d), const(d, d),
                  const(N_BRANCH, d), const(1, d)],
        out_specs=zblk(0),
        out_shape=jax.ShapeDtypeStruct((t, d), F32),
        compiler_params=_params("parallel", "arbitrary"),
        name="mix_out",
    )(z, z, z, z, z, kv, yg, x, pool_w, pool_b, pool_scale, w_proj_pool, w_proj_mem, w_proj_rwkv,
      w_o, gate_b, g_post)


def _rwkv_in_kernel(*refs, tm, tiles_per_seq, has_vres):
    (x_ref, gn_ref, w_ref, mu_ref, w0_ref, wdec_ref, a0_ref, wa_ref, wg_ref, kk_ref, ka_ref) = refs[:11]
    refs = refs[11:]
    if has_vres:
        vfirst_ref, v0_ref, wdv_ref, wuv_ref = refs[:4]
        refs = refs[4:]
    r_o, lw_o, k_o, v_o, kk_o, b_o, g_o, carry_ref = refs
    d = RWKV_WIDTH

    xn = _rms(x_ref[...], gn_ref[...]).astype(BF16)
    first = (pl.program_id(0) % tiles_per_seq) == 0

    def project(lo, hi):
        z = _dot(xn, w_ref[:, lo:hi])
        last = jnp.where(first, 0.0, carry_ref[SUBLANES - 1:SUBLANES, lo:hi])
        carry_ref[:, lo:hi] = z[tm - SUBLANES:, :]
        row = lax.broadcasted_iota(jnp.int32, z.shape, 0)
        zp = jnp.where(row == 0, last, pltpu.roll(z, 1, axis=0))
        return z + (zp - z) * mu_ref[:, lo:hi]

    lora = project(3 * d, RWKV_COLS)
    dwa = lora[:, :LANES]
    dg = lora[:, LANES:]
    r_o[...] = project(0, d).astype(r_o.dtype)
    k = project(d, 2 * d)

    u = w0_ref[...] + _dot(jnp.tanh(dwa).astype(BF16), wdec_ref[...])
    lw_o[...] = -math.exp(-0.5) * _sigmoid(u)
    alpha = _sigmoid(a0_ref[...] + _dot(dwa.astype(BF16), wa_ref[...]))
    g_o[...] = _dot(_sigmoid(dg).astype(BF16), wg_ref[...]).astype(g_o.dtype)

    v = project(2 * d, 3 * d)

    seg = _seg_matrix()
    kk = k * kk_ref[...]
    kk2 = kk * kk
    ss = jnp.concatenate(_head_sums([kk2[:, s * LANES:(s + 1) * LANES] for s in range(N_SLABS)], seg), axis=1)
    kk = kk * lax.rsqrt(ss + L2_EPS)
    k_o[...] = (k * (1.0 + (alpha - 1.0) * ka_ref[...])).astype(k_o.dtype)
    kk_o[...] = kk.astype(kk_o.dtype)
    b_o[...] = (kk * alpha).astype(b_o.dtype)

    if has_vres:
        low = _dot(v.astype(BF16), wdv_ref[...])
        vg = _sigmoid(v0_ref[...] + _dot(low.astype(BF16), wuv_ref[...]))
        v = v + (vfirst_ref[...].astype(F32) - v) * vg
    v_o[...] = v.astype(v_o.dtype)


def _rwkv_in(x, g_pre, w, mu, w0, wdec, a0, wa, wg, k_k, k_a, vres, seq, tm):
    t, d = x.shape
    const = lambda *shape: pl.BlockSpec(shape, lambda i: (0,) * len(shape))
    blk = pl.BlockSpec((tm, d), lambda i: (i, 0))
    in_specs = [blk, const(1, d), const(d, RWKV_COLS), const(1, RWKV_COLS), const(1, d),
                const(LANES, d), const(1, d), const(LANES, d), const(GATE_LORA, d), const(1, d),
                const(1, d)]
    args = [x, g_pre, w, mu, w0, wdec, a0, wa, wg, k_k, k_a]
    if vres is not None:
        in_specs += [blk, const(1, d), const(d, LANES), const(LANES, d)]
        args += list(vres)
    return pl.pallas_call(
        functools.partial(_rwkv_in_kernel, tm=tm, tiles_per_seq=seq // tm, has_vres=vres is not None),
        grid=(t // tm,),
        in_specs=in_specs,
        out_specs=[blk] * 7,
        out_shape=[jax.ShapeDtypeStruct((t, d), F32 if n == 1 else BF16) for n in range(7)],
        scratch_shapes=[pltpu.VMEM((SUBLANES, RWKV_COLS), F32)],
        compiler_params=_params("arbitrary"),
        name="rwkv_in",
    )(*args)


def _stack_heads(x, m1):
    return jnp.concatenate([jnp.where(m1, x, 0.0), jnp.where(m1, 0.0, x)], axis=0)


def _scan_kernel(r_ref, lw_ref, k_ref, v_ref, kk_ref, b_ref, g_ref, rk_ref, lnw_ref, lnb_ref,
                 o_ref, h_ref):
    c = SCAN_CHUNK
    c2 = HEADS_PER_SLAB * c

    @pl.when(pl.program_id(1) == 0)
    def _():
        h_ref[...] = jnp.zeros_like(h_ref)

    lane = lax.broadcasted_iota(jnp.int32, (c, LANES), 1)
    m1 = lane < RWKV_HEAD_DIM
    ti = lax.broadcasted_iota(jnp.int32, (c, LANES), 0)
    tj = lane % RWKV_HEAD_DIM
    strict = ti > tj
    incl = ti >= tj
    eye = jnp.where(ti == tj, 1.0, 0.0)
    seg = _seg_matrix()
    inv_n = 1.0 / RWKV_HEAD_DIM
    ltri = jnp.where(lax.broadcasted_iota(jnp.int32, (c, c), 0)
                     >= lax.broadcasted_iota(jnp.int32, (c, c), 1), 1.0, 0.0).astype(BF16)

    units = [(j, slice(s * LANES, (s + 1) * LANES)) for j in range(SCAN_SEQS) for s in range(N_SLABS)]
    nu = range(len(units))

    lws, cums = [], []
    for j in range(SCAN_SEQS):
        lw = lw_ref[j]
        hi = lw.astype(BF16)
        r1 = lw - hi.astype(F32)
        mid = r1.astype(BF16)
        lo = (r1 - mid.astype(F32)).astype(BF16)
        lws.append(lw)
        cums.append(_dot(ltri, hi) + _dot(ltri, mid) + _dot(ltri, lo))

    ar, d_ak, d_r, tinv, apow, bk, decay_c = [], [], [], [], [], [], []
    for j, sl in units:
        cum_s = cums[j][:, sl]
        last = cum_s[c - 1:c, :]
        p_inv = jnp.exp(-cum_s)
        p_end = jnp.exp(last - cum_s)
        kx = k_ref[j, :, sl].astype(F32)
        bb = b_ref[j, :, sl].astype(F32)
        a_t = -kk_ref[j, :, sl].astype(F32) * jnp.exp(cum_s - lws[j][:, sl])
        r_t = r_ref[j, :, sl].astype(F32) * jnp.exp(cum_s)
        ar_s = jnp.concatenate([a_t, r_t], axis=0).astype(BF16)
        xr = jnp.concatenate([_stack_heads(bb * p_inv, m1), _stack_heads(kx * p_inv, m1)],
                             axis=0).astype(BF16)
        gm = _dot_nt(ar_s, xr)
        d_ab = jnp.where(strict, gm[:c, :c2], 0.0)
        d_ak.append(jnp.where(strict, gm[:c, c2:], 0.0).astype(BF16))
        d_r.append(jnp.concatenate([jnp.where(incl, gm[c:, :c2], 0.0),
                                    jnp.where(incl, gm[c:, c2:], 0.0)], axis=1).astype(BF16))
        tinv.append(eye + d_ab)
        apow.append(d_ab.astype(BF16))
        ar.append(ar_s)
        bk.append(jnp.concatenate([_stack_heads(bb * p_end, m1), _stack_heads(kx * p_end, m1)],
                                  axis=0).astype(BF16))
        decay_c.append(jnp.exp(last))

    for i in nu:
        apow[i] = _dot(apow[i], _stack_heads(apow[i], m1)).astype(BF16)
    n = 4
    while n < c:
        for i in nu:
            both = _dot(jnp.concatenate([tinv[i].astype(BF16), apow[i]], axis=0),
                        _stack_heads(apow[i], m1))
            tinv[i] = tinv[i] + both[:c]
            apow[i] = both[c:].astype(BF16)
        n *= 2
    for i in nu:
        tinv[i] = (tinv[i] + _dot(tinv[i].astype(BF16), _stack_heads(apow[i], m1))).astype(BF16)

    hs = [h_ref[i] for i in nu]
    xh = [_dot_nt(ar[i], hs[i].astype(BF16)) for i in nu]
    v_st = [_stack_heads(v_ref[j, :, sl], m1).astype(BF16) for j, sl in units]
    z = [xh[i][:c] + _dot(d_ak[i], v_st[i]) for i in nu]
    u = [_dot(tinv[i], _stack_heads(z[i], m1).astype(BF16)) for i in nu]
    uv = [jnp.concatenate([_stack_heads(u[i], m1).astype(BF16), v_st[i]], axis=0) for i in nu]
    y = [xh[i][c:] + _dot(d_r[i], uv[i]) for i in nu]
    for i in nu:
        h_ref[i] = hs[i] * decay_c[i] + _dot_tn(uv[i], bk[i])

    bonus = _head_sums([r_ref[j, :, sl].astype(F32) * k_ref[j, :, sl].astype(F32) * rk_ref[:, sl]
                        for j, sl in units], seg)
    mu = _head_sums(y, seg)
    dlt = [y[i] - mu[i] * inv_n for i in nu]
    var = _head_sums([dlt[i] * dlt[i] for i in nu], seg)
    for i, (j, sl) in enumerate(units):
        yn = dlt[i] * lax.rsqrt(var[i] * inv_n + LNX_EPS) * lnw_ref[:, sl] + lnb_ref[:, sl]
        o_ref[j, :, sl] = ((yn + bonus[i] * v_ref[j, :, sl].astype(F32))
                           * g_ref[j, :, sl].astype(F32)).astype(BF16)


def _scan(r, lw, k, v, kk, b, g, r_k, ln_w, ln_b, batch, seq):
    t, d = r.shape
    c = SCAN_CHUNK
    as_seqs = lambda a: a.reshape(batch, seq, d)
    blk = pl.BlockSpec((SCAN_SEQS, c, d), lambda bi, ci: (bi, ci, 0))
    row = pl.BlockSpec((1, d), lambda bi, ci: (0, 0))
    out = pl.pallas_call(
        _scan_kernel,
        grid=(batch // SCAN_SEQS, seq // c),
        in_specs=[blk] * 7 + [row] * 3,
        out_specs=blk,
        out_shape=jax.ShapeDtypeStruct((batch, seq, d), BF16),
        scratch_shapes=[pltpu.VMEM((SCAN_SEQS * N_SLABS, LANES, LANES), F32)],
        compiler_params=_params("parallel", "arbitrary"),
        name="rwkv_scan",
    )(*(as_seqs(a) for a in (r, lw, k, v, kk, b, g)), r_k, ln_w, ln_b)
    return out.reshape(t, d)


GELU_C0 = math.sqrt(2.0 / math.pi)
GELU_C1 = GELU_C0 * 0.044715


def _gelu_tanh(x):
    hx = 0.5 * x
    return hx + hx * jnp.tanh(x * (GELU_C0 + GELU_C1 * (x * x)))


def _ffn_kernel(x_ref, gpre_ref, wup_ref, cw_ref, cb_ref, wd_ref, gpost_ref, o_ref,
                ubuf_ref, f_ref, carry_ref, *, tm, tiles_per_seq):
    first = (pl.program_id(0) % tiles_per_seq) == 0
    fc = FFN_COLS
    hn = _rms(x_ref[...], gpre_ref[...]).astype(BF16)
    for j in range(D_FF // fc):
        halves = []
        for half in range(2):
            cols = slice(half * D_FF + j * fc, half * D_FF + (j + 1) * fc)
            buf = ubuf_ref.at[2 * (j % 2) + half]
            u = _dot(hn, wup_ref[:, cols])
            buf[0:SUBLANES, :] = jnp.where(first, 0.0, carry_ref[:, cols])
            buf[SUBLANES:, :] = u
            carry_ref[:, cols] = u[tm - SUBLANES:, :]
            out = cb_ref[:, cols] + cw_ref[0:1, cols] * buf[SUBLANES - 2:SUBLANES - 2 + tm, :]
            out = out + cw_ref[1:2, cols] * buf[SUBLANES - 1:SUBLANES - 1 + tm, :]
            halves.append(out + cw_ref[2:3, cols] * u)
        f_ref[:, j * fc:(j + 1) * fc] = (_gelu_tanh(halves[0]) * halves[1]).astype(BF16)
    acc = _dot(f_ref[...], wd_ref[...])
    o_ref[...] = x_ref[...] + _rms(acc, gpost_ref[...])


def _ffn(x, g_pre, w_up, conv_w, conv_b, w_down, g_post, seq, tm):
    t, d = x.shape
    xblk = pl.BlockSpec((tm, d), lambda i: (i, 0))
    const = lambda *shape: pl.BlockSpec(shape, lambda i: (0,) * len(shape))
    return pl.pallas_call(
        functools.partial(_ffn_kernel, tm=tm, tiles_per_seq=seq // tm),
        grid=(t // tm,),
        in_specs=[xblk, const(1, d), const(d, 2 * D_FF), const(CONV_WIDTH, 2 * D_FF),
                  const(1, 2 * D_FF), const(D_FF, d), const(1, d)],
        out_specs=xblk,
        out_shape=jax.ShapeDtypeStruct((t, d), F32),
        scratch_shapes=[pltpu.VMEM((4, SUBLANES + tm, FFN_COLS), F32),
                        pltpu.VMEM((tm, D_FF), BF16),
                        pltpu.VMEM((SUBLANES, 2 * D_FF), F32)],
        compiler_params=_params("arbitrary"),
        name="conv_ffn",
    )(x, g_pre, w_up, conv_w, conv_b, w_down, g_post)


def _pad_rows(w, before, total):
    return jnp.pad(w, ((before, total - before - w.shape[0]), (0, 0)))


def kernel(x, mem, mem_norm, norm_mix_pre, norm_mix_post, w_in, mu_shift, pool_w, pool_b,
           pool_scale, w_proj_pool, w_mem_kv, w_proj_mem, w0, w_up_decay, a0, w_up_a, w_up_g,
           k_k, k_a, r_k, ln_x_w, ln_x_b, v0, w_down_v, w_up_v, w_proj_rwkv, gate_b, w_o,
           norm_ffn_pre, norm_ffn_post, w_ffn_up, conv_w, conv_b, w_ffn_down):
    batch, seq, d = x.shape
    depth = w_in.shape[0]
    assert d == D_MODEL and seq % 512 == 0 and mem.shape[1] == MEM_LEN and batch % SCAN_SEQS == 0
    t = batch * seq
    xf = x.reshape(t, d)
    memf = mem.reshape(batch * MEM_LEN, d)
    row = lambda a: a.reshape(1, -1)
    bf = lambda a: a.astype(BF16)

    v_first = None
    for l in range(depth):
        w_pqg = jnp.concatenate([w_in[l][:, :OFF_RWKV], w_in[l][:, OFF_GATE:]], axis=1)
        z = _norm_matmul(xf, row(norm_mix_pre[l]), bf(w_pqg), 512, w_pqg.shape[1] // 2)
        kv = _norm_matmul(memf, row(mem_norm), bf(w_mem_kv[l]), 512, 2 * MEM_WIDTH)
        vres = None
        if l > 0:
            vres = (v_first, row(v0[l - 1]),
                    bf(jnp.pad(w_down_v[l - 1], ((0, 0), (0, LANES - VRES_LORA)))),
                    bf(_pad_rows(w_up_v[l - 1], 0, LANES)))
        r, lw, k, v, kk, b, g = _rwkv_in(
            xf, row(norm_mix_pre[l]), bf(w_in[l][:, OFF_RWKV:OFF_GATE]), row(mu_shift[l]), row(w0[l]),
            bf(_pad_rows(w_up_decay[l], 0, LANES)), row(a0[l]),
            bf(_pad_rows(w_up_a[l], DECAY_LORA, LANES)), bf(w_up_g[l]), row(k_k[l]), row(k_a[l]),
            vres, seq, 256)
        if l == 0:
            v_first = v
        yg = _scan(r, lw, k, v, kk, b, g, row(r_k[l]), row(ln_x_w[l]), row(ln_x_b[l]), batch, seq)
        xf = _mix_out(z, kv, yg, xf, bf(pool_w[l]), row(pool_b[l]), row(pool_scale[l]),
                      bf(w_proj_pool[l]), bf(w_proj_mem[l]), bf(w_proj_rwkv[l]), bf(w_o[l]), gate_b[l],
                      row(norm_mix_post[l]), batch, seq, 512)
        xf = _ffn(xf, row(norm_ffn_pre[l]), bf(w_ffn_up[l]), conv_w[l], row(conv_b[l]),
                  bf(w_ffn_down[l]), row(norm_ffn_post[l]), seq, 512)
    return xf.reshape(batch, seq, d)
```

```python
import functools
import math

import jax
import jax.numpy as jnp
from jax import lax
from jax.experimental import pallas as pl
from jax.experimental.pallas import tpu as pltpu

F32 = jnp.float32
BF16 = jnp.bfloat16

D_MODEL = 1024
MEM_LEN = 256
POOL_WIDTH = 512
POOL_GROUP_DIM = 128
POOL_WINDOWS = (2, 4, 8, 16)
POOL_HALO = 16
MEM_HEADS = 4
MEM_HEAD_DIM = 128
MEM_WIDTH = MEM_HEADS * MEM_HEAD_DIM
RWKV_HEAD_DIM = 64
RWKV_WIDTH = D_MODEL
DECAY_LORA = 64
ICLR_LORA = 64
VRES_LORA = 32
GATE_LORA = 128
LORA_COLS = DECAY_LORA + ICLR_LORA + GATE_LORA
RWKV_COLS = 3 * RWKV_WIDTH + LORA_COLS
N_BRANCH = 3
OFF_Q = POOL_WIDTH
OFF_RWKV = OFF_Q + MEM_WIDTH
OFF_GATE = OFF_RWKV + RWKV_COLS
IN_COLS = OFF_GATE + N_BRANCH * D_MODEL
D_FF = 2816
CONV_WIDTH = 3
NORM_EPS = 1e-6
LNX_EPS = 64e-5
L2_EPS = 1e-12

LANES = 128
SUBLANES = 8
VMEM_LIMIT_BYTES = 56 * 1024 * 1024

SCAN_CHUNK = 64
SCAN_SEQS = 4
HEADS_PER_SLAB = LANES // RWKV_HEAD_DIM
N_SLABS = RWKV_WIDTH // LANES
FFN_COLS = 256
FFN_DOWN_SPLITS = ((0, 4), (4, 8), (8, 11))

ZBLK_PQ, ZBLK_G0, ZBLK_G1, ZBLK_G2 = range(4)


def _params(*sem):
    return pltpu.CompilerParams(dimension_semantics=sem, vmem_limit_bytes=VMEM_LIMIT_BYTES)


def _dot(a, b):
    return jnp.dot(a, b, preferred_element_type=F32)


def _dot_nt(a, b):
    return lax.dot_general(a, b, (((1,), (1,)), ((), ())), preferred_element_type=F32)


def _dot_tn(a, b):
    return lax.dot_general(a, b, (((0,), (0,)), ((), ())), preferred_element_type=F32)


def _sigmoid(x):
    return 1.0 / (1.0 + jnp.exp(-x))


def _rms(x, g):
    ms = jnp.mean(x * x, axis=-1, keepdims=True)
    return x * lax.rsqrt(ms + NORM_EPS) * g


def _split2(x):
    hi = x.astype(BF16)
    lo = (x - hi.astype(F32)).astype(BF16)
    return hi, lo


def _head_sums(xs, seg):
    rows = xs[0].shape[0]
    parts = [p for x in xs for p in _split2(x)]
    res = _dot(jnp.concatenate(parts, axis=0), seg)
    return [res[2 * i * rows:(2 * i + 1) * rows] + res[(2 * i + 1) * rows:(2 * i + 2) * rows]
            for i in range(len(xs))]


def _seg_matrix():
    i = lax.broadcasted_iota(jnp.int32, (LANES, LANES), 0) // RWKV_HEAD_DIM
    j = lax.broadcasted_iota(jnp.int32, (LANES, LANES), 1) // RWKV_HEAD_DIM
    return jnp.where(i == j, 1.0, 0.0).astype(BF16)


def _norm_matmul_kernel(x_ref, g_ref, w_ref, o_ref):
    o_ref[...] = _dot(_rms(x_ref[...], g_ref[...]).astype(BF16), w_ref[...]).astype(o_ref.dtype)


def _norm_matmul(x, g, w, tm, tn):
    t, d = x.shape
    n = w.shape[1]
    return pl.pallas_call(
        _norm_matmul_kernel,
        grid=(n // tn, t // tm),
        in_specs=[pl.BlockSpec((tm, d), lambda j, i: (i, 0)),
                  pl.BlockSpec((1, d), lambda j, i: (0, 0)),
                  pl.BlockSpec((d, tn), lambda j, i: (0, j))],
        out_specs=pl.BlockSpec((tm, tn), lambda j, i: (i, j)),
        out_shape=jax.ShapeDtypeStruct((t, n), BF16),
        compiler_params=_params("parallel", "parallel"),
        name="norm_matmul",
    )(x, g, w)


def _mix_out_kernel(zpq_ref, halo_ref, zg0_ref, zg1_ref, zg2_ref, kv_ref, yg_ref, x_ref,
                    poolw_ref, poolb_ref, pools_ref, wpp_ref, wpm_ref, wpr_ref, wo_ref, gb_ref,
                    gn_ref, o_ref, *, tm):
    i = pl.program_id(1)
    hd = [slice(h * MEM_HEAD_DIM, (h + 1) * MEM_HEAD_DIM) for h in range(MEM_HEADS)]
    scores = [_dot_nt(zpq_ref[:, POOL_WIDTH + sl.start:POOL_WIDTH + sl.stop].astype(BF16),
                      kv_ref[:, sl].astype(BF16)) * (MEM_HEAD_DIM ** -0.5) for sl in hd]
    y_rwkv = _dot(yg_ref[...], wpr_ref[...])

    p = zpq_ref[:, :POOL_WIDTH].astype(F32)
    prev = jnp.where(i == 0, 0.0, halo_ref[...].astype(F32))
    pe = jnp.concatenate([prev, p], axis=0)
    pos = lax.broadcasted_iota(jnp.int32, (tm, POOL_GROUP_DIM), 0) + i * tm
    mixed = []
    for gi, win in enumerate(POOL_WINDOWS):
        sl = slice(gi * POOL_GROUP_DIM, (gi + 1) * POOL_GROUP_DIM)
        s = pe[:, sl]
        k = 1
        while k < win:
            s = s + pltpu.roll(s, k, axis=0)
            k *= 2
        cnt = jnp.minimum(pos + 1, win).astype(F32)
        pooled = s[POOL_HALO:] / cnt - p[:, sl]
        m = _dot(pooled.astype(BF16), poolw_ref[gi]) + poolb_ref[:, sl]
        mixed.append((m * pools_ref[:, sl]).astype(BF16))
    y_pool = _dot(jnp.concatenate(mixed, axis=1), wpp_ref[...])

    probs = []
    for s in scores:
        e = jnp.exp(s - jnp.max(s, axis=-1, keepdims=True))
        probs.append((e / jnp.sum(e, axis=-1, keepdims=True)).astype(BF16))
    heads = [_dot(probs[h], kv_ref[:, MEM_WIDTH + sl.start:MEM_WIDTH + sl.stop].astype(BF16)).astype(BF16)
             for h, sl in enumerate(hd)]
    y_mem = _dot(jnp.concatenate(heads, axis=1), wpm_ref[...])

    gate = lambda z_ref, n: _sigmoid(z_ref[...].astype(F32) + gb_ref[n:n + 1, :])
    merged = gate(zg0_ref, 0) * y_pool + gate(zg1_ref, 1) * y_rwkv + gate(zg2_ref, 2) * y_mem
    o = _dot(merged.astype(BF16), wo_ref[...])
    o_ref[...] = x_ref[...] + _rms(o, gn_ref[...])


def _mix_out(z, kv, yg, x, pool_w, pool_b, pool_scale, w_proj_pool, w_proj_mem, w_proj_rwkv, w_o,
             gate_b, g_post, batch, seq, tm):
    t, d = x.shape
    nt = seq // tm
    hb = tm // POOL_HALO
    const = lambda *shape: pl.BlockSpec(shape, lambda b, i: (0,) * len(shape))
    zblk = lambda blk: pl.BlockSpec((tm, d), lambda b, i: (b * nt + i, blk))
    return pl.pallas_call(
        functools.partial(_mix_out_kernel, tm=tm),
        grid=(batch, nt),
        in_specs=[zblk(ZBLK_PQ),
                  pl.BlockSpec((POOL_HALO, POOL_WIDTH),
                               lambda b, i: (jnp.maximum((b * nt + i) * hb - 1, 0), 0)),
                  zblk(ZBLK_G0), zblk(ZBLK_G1), zblk(ZBLK_G2),
                  pl.BlockSpec((MEM_LEN, 2 * MEM_WIDTH), lambda b, i: (b, 0)),
                  zblk(0), zblk(0),
                  const(len(POOL_WINDOWS), POOL_GROUP_DIM, POOL_GROUP_DIM),
                  const(1, POOL_WIDTH), const(1, POOL_WIDTH),
                  const(POOL_WIDTH, d), const(MEM_WIDTH, d), const(d, d), const(d, d),
                  const(N_BRANCH, d), const(1, d)],
        out_specs=zblk(0),
        out_shape=jax.ShapeDtypeStruct((t, d), F32),
        compiler_params=_params("parallel", "arbitrary"),
        name="mix_out",
    )(z, z, z, z, z, kv, yg, x, pool_w, pool_b, pool_scale, w_proj_pool, w_proj_mem, w_proj_rwkv,
      w_o, gate_b, g_post)


def _rwkv_in_kernel(*refs, tm, tiles_per_seq, has_vres):
    (x_ref, gn_ref, w_ref, mu_ref, w0_ref, wdec_ref, a0_ref, wa_ref, wg_ref, kk_ref, ka_ref) = refs[:11]
    refs = refs[11:]
    if has_vres:
        vfirst_ref, v0_ref, wdv_ref, wuv_ref = refs[:4]
        refs = refs[4:]
    r_o, lw_o, k_o, v_o, kk_o, b_o, g_o, carry_ref = refs
    d = RWKV_WIDTH

    xn = _rms(x_ref[...], gn_ref[...]).astype(BF16)
    first = (pl.program_id(0) % tiles_per_seq) == 0

    def project(lo, hi):
        z = _dot(xn, w_ref[:, lo:hi])
        last = jnp.where(first, 0.0, carry_ref[SUBLANES - 1:SUBLANES, lo:hi])
        carry_ref[:, lo:hi] = z[tm - SUBLANES:, :]
        row = lax.broadcasted_iota(jnp.int32, z.shape, 0)
        zp = jnp.where(row == 0, last, pltpu.roll(z, 1, axis=0))
        return z + (zp - z) * mu_ref[:, lo:hi]

    lora = project(3 * d, RWKV_COLS)
    dwa = lora[:, :LANES]
    dg = lora[:, LANES:]
    r_o[...] = project(0, d).astype(r_o.dtype)
    k = project(d, 2 * d)

    u = w0_ref[...] + _dot(jnp.tanh(dwa).astype(BF16), wdec_ref[...])
    lw_o[...] = -math.exp(-0.5) * _sigmoid(u)
    alpha = _sigmoid(a0_ref[...] + _dot(dwa.astype(BF16), wa_ref[...]))
    g_o[...] = _dot(_sigmoid(dg).astype(BF16), wg_ref[...]).astype(g_o.dtype)

    v = project(2 * d, 3 * d)

    seg = _seg_matrix()
    kk = k * kk_ref[...]
    kk2 = kk * kk
    ss = jnp.concatenate(_head_sums([kk2[:, s * LANES:(s + 1) * LANES] for s in range(N_SLABS)], seg), axis=1)
    kk = kk * lax.rsqrt(ss + L2_EPS)
    k_o[...] = (k * (1.0 + (alpha - 1.0) * ka_ref[...])).astype(k_o.dtype)
    kk_o[...] = kk.astype(kk_o.dtype)
    b_o[...] = (kk * alpha).astype(b_o.dtype)

    if has_vres:
        low = _dot(v.astype(BF16), wdv_ref[...])
        vg = _sigmoid(v0_ref[...] + _dot(low.astype(BF16), wuv_ref[...]))
        v = v + (vfirst_ref[...].astype(F32) - v) * vg
    v_o[...] = v.astype(v_o.dtype)


def _rwkv_in(x, g_pre, w, mu, w0, wdec, a0, wa, wg, k_k, k_a, vres, seq, tm):
    t, d = x.shape
    const = lambda *shape: pl.BlockSpec(shape, lambda i: (0,) * len(shape))
    blk = pl.BlockSpec((tm, d), lambda i: (i, 0))
    in_specs = [blk, const(1, d), const(d, RWKV_COLS), const(1, RWKV_COLS), const(1, d),
                const(LANES, d), const(1, d), const(LANES, d), const(GATE_LORA, d), const(1, d),
                const(1, d)]
    args = [x, g_pre, w, mu, w0, wdec, a0, wa, wg, k_k, k_a]
    if vres is not None:
        in_specs += [blk, const(1, d), const(d, LANES), const(LANES, d)]
        args += list(vres)
    return pl.pallas_call(
        functools.partial(_rwkv_in_kernel, tm=tm, tiles_per_seq=seq // tm, has_vres=vres is not None),
        grid=(t // tm,),
        in_specs=in_specs,
        out_specs=[blk] * 7,
        out_shape=[jax.ShapeDtypeStruct((t, d), F32 if n == 1 else BF16) for n in range(7)],
        scratch_shapes=[pltpu.VMEM((SUBLANES, RWKV_COLS), F32)],
        compiler_params=_params("arbitrary"),
        name="rwkv_in",
    )(*args)


def _stack_heads(x, m1):
    return jnp.concatenate([jnp.where(m1, x, 0.0), jnp.where(m1, 0.0, x)], axis=0)


def _scan_kernel(r_ref, lw_ref, k_ref, v_ref, kk_ref, b_ref, g_ref, rk_ref, lnw_ref, lnb_ref,
                 o_ref, h_ref):
    c = SCAN_CHUNK
    c2 = HEADS_PER_SLAB * c

    @pl.when(pl.program_id(1) == 0)
    def _():
        h_ref[...] = jnp.zeros_like(h_ref)

    lane = lax.broadcasted_iota(jnp.int32, (c, LANES), 1)
    m1 = lane < RWKV_HEAD_DIM
    ti = lax.broadcasted_iota(jnp.int32, (c, LANES), 0)
    tj = lane % RWKV_HEAD_DIM
    strict = ti > tj
    incl = ti >= tj
    eye = jnp.where(ti == tj, 1.0, 0.0)
    seg = _seg_matrix()
    inv_n = 1.0 / RWKV_HEAD_DIM
    ltri = jnp.where(lax.broadcasted_iota(jnp.int32, (c, c), 0)
                     >= lax.broadcasted_iota(jnp.int32, (c, c), 1), 1.0, 0.0).astype(BF16)

    units = [(j, slice(s * LANES, (s + 1) * LANES)) for j in range(SCAN_SEQS) for s in range(N_SLABS)]
    nu = range(len(units))

    lws, cums = [], []
    for j in range(SCAN_SEQS):
        lw = lw_ref[j]
        hi = lw.astype(BF16)
        r1 = lw - hi.astype(F32)
        mid = r1.astype(BF16)
        lo = (r1 - mid.astype(F32)).astype(BF16)
        lws.append(lw)
        cums.append(_dot(ltri, hi) + _dot(ltri, mid) + _dot(ltri, lo))

    ar, d_ak, d_r, tinv, apow, bk, decay_c = [], [], [], [], [], [], []
    for j, sl in units:
        cum_s = cums[j][:, sl]
        last = cum_s[c - 1:c, :]
        p_inv = jnp.exp(-cum_s)
        p_end = jnp.exp(last - cum_s)
        kx = k_ref[j, :, sl].astype(F32)
        bb = b_ref[j, :, sl].astype(F32)
        a_t = -kk_ref[j, :, sl].astype(F32) * jnp.exp(cum_s - lws[j][:, sl])
        r_t = r_ref[j, :, sl].astype(F32) * jnp.exp(cum_s)
        ar_s = jnp.concatenate([a_t, r_t], axis=0).astype(BF16)
        xr = jnp.concatenate([_stack_heads(bb * p_inv, m1), _stack_heads(kx * p_inv, m1)],
                             axis=0).astype(BF16)
        gm = _dot_nt(ar_s, xr)
        d_ab = jnp.where(strict, gm[:c, :c2], 0.0)
        d_ak.append(jnp.where(strict, gm[:c, c2:], 0.0).astype(BF16))
        d_r.append(jnp.concatenate([jnp.where(incl, gm[c:, :c2], 0.0),
                                    jnp.where(incl, gm[c:, c2:], 0.0)], axis=1).astype(BF16))
        tinv.append(eye + d_ab)
        apow.append(d_ab.astype(BF16))
        ar.append(ar_s)
        bk.append(jnp.concatenate([_stack_heads(bb * p_end, m1), _stack_heads(kx * p_end, m1)],
                                  axis=0).astype(BF16))
        decay_c.append(jnp.exp(last))

    for i in nu:
        apow[i] = _dot(apow[i], _stack_heads(apow[i], m1)).astype(BF16)
    n = 4
    while n < c:
        for i in nu:
            both = _dot(jnp.concatenate([tinv[i].astype(BF16), apow[i]], axis=0),
                        _stack_heads(apow[i], m1))
            tinv[i] = tinv[i] + both[:c]
            apow[i] = both[c:].astype(BF16)
        n *= 2
    for i in nu:
        tinv[i] = (tinv[i] + _dot(tinv[i].astype(BF16), _stack_heads(apow[i], m1))).astype(BF16)

    hs = [h_ref[i] for i in nu]
    xh = [_dot_nt(ar[i], hs[i].astype(BF16)) for i in nu]
    v_st = [_stack_heads(v_ref[j, :, sl], m1).astype(BF16) for j, sl in units]
    z = [xh[i][:c] + _dot(d_ak[i], v_st[i]) for i in nu]
    u = [_dot(tinv[i], _stack_heads(z[i], m1).astype(BF16)) for i in nu]
    uv = [jnp.concatenate([_stack_heads(u[i], m1).astype(BF16), v_st[i]], axis=0) for i in nu]
    y = [xh[i][c:] + _dot(d_r[i], uv[i]) for i in nu]
    for i in nu:
        h_ref[i] = hs[i] * decay_c[i] + _dot_tn(uv[i], bk[i])

    bonus = _head_sums([r_ref[j, :, sl].astype(F32) * k_ref[j, :, sl].astype(F32) * rk_ref[:, sl]
                        for j, sl in units], seg)
    mu = _head_sums(y, seg)
    dlt = [y[i] - mu[i] * inv_n for i in nu]
    var = _head_sums([dlt[i] * dlt[i] for i in nu], seg)
    for i, (j, sl) in enumerate(units):
        yn = dlt[i] * lax.rsqrt(var[i] * inv_n + LNX_EPS) * lnw_ref[:, sl] + lnb_ref[:, sl]
        o_ref[j, :, sl] = ((yn + bonus[i] * v_ref[j, :, sl].astype(F32))
                           * g_ref[j, :, sl].astype(F32)).astype(BF16)


def _scan(r, lw, k, v, kk, b, g, r_k, ln_w, ln_b, batch, seq):
    t, d = r.shape
    c = SCAN_CHUNK
    as_seqs = lambda a: a.reshape(batch, seq, d)
    blk = pl.BlockSpec((SCAN_SEQS, c, d), lambda bi, ci: (bi, ci, 0))
    row = pl.BlockSpec((1, d), lambda bi, ci: (0, 0))
    out = pl.pallas_call(
        _scan_kernel,
        grid=(batch // SCAN_SEQS, seq // c),
        in_specs=[blk] * 7 + [row] * 3,
        out_specs=blk,
        out_shape=jax.ShapeDtypeStruct((batch, seq, d), BF16),
        scratch_shapes=[pltpu.VMEM((SCAN_SEQS * N_SLABS, LANES, LANES), F32)],
        compiler_params=_params("parallel", "arbitrary"),
        name="rwkv_scan",
    )(*(as_seqs(a) for a in (r, lw, k, v, kk, b, g)), r_k, ln_w, ln_b)
    return out.reshape(t, d)


GELU_C0 = math.sqrt(2.0 / math.pi)
GELU_C1 = GELU_C0 * 0.044715


def _gelu_tanh(x):
    hx = 0.5 * x
    return hx + hx * jnp.tanh(x * (GELU_C0 + GELU_C1 * (x * x)))


def _ffn_kernel(x_ref, gpre_ref, wup_ref, cw_ref, cb_ref, wd_ref, gpost_ref, o_ref,
                ubuf_ref, f_ref, carry_ref, *, tm, tiles_per_seq):
    first = (pl.program_id(0) % tiles_per_seq) == 0
    fc = FFN_COLS
    nf = D_FF // fc
    hn = _rms(x_ref[...], gpre_ref[...]).astype(BF16)
    for j in range(nf):
        halves = []
        for half in range(2):
            cols = slice(half * D_FF + j * fc, half * D_FF + (j + 1) * fc)
            buf = ubuf_ref.at[2 * (j % 2) + half]
            u = _dot(hn, wup_ref[:, cols])
            buf[0:SUBLANES, :] = jnp.where(first, 0.0, carry_ref[:, cols])
            buf[SUBLANES:, :] = u
            carry_ref[:, cols] = u[tm - SUBLANES:, :]
            out = cb_ref[:, cols] + cw_ref[0:1, cols] * buf[SUBLANES - 2:SUBLANES - 2 + tm, :]
            out = out + cw_ref[1:2, cols] * buf[SUBLANES - 1:SUBLANES - 1 + tm, :]
            halves.append(out + cw_ref[2:3, cols] * u)
        f_ref[:, j * fc:(j + 1) * fc] = (_gelu_tanh(halves[0]) * halves[1]).astype(BF16)
    acc = None
    for lo, hi in FFN_DOWN_SPLITS:
        part = _dot(f_ref[:, lo * fc:hi * fc], wd_ref[lo * fc:hi * fc, :])
        acc = part if acc is None else part + acc
    o_ref[...] = x_ref[...] + _rms(acc, gpost_ref[...])


def _ffn(x, g_pre, w_up, conv_w, conv_b, w_down, g_post, seq, tm):
    t, d = x.shape
    xblk = pl.BlockSpec((tm, d), lambda i: (i, 0))
    const = lambda *shape: pl.BlockSpec(shape, lambda i: (0,) * len(shape))
    return pl.pallas_call(
        functools.partial(_ffn_kernel, tm=tm, tiles_per_seq=seq // tm),
        grid=(t // tm,),
        in_specs=[xblk, const(1, d), const(d, 2 * D_FF), const(CONV_WIDTH, 2 * D_FF),
                  const(1, 2 * D_FF), const(D_FF, d), const(1, d)],
        out_specs=xblk,
        out_shape=jax.ShapeDtypeStruct((t, d), F32),
        scratch_shapes=[pltpu.VMEM((4, SUBLANES + tm, FFN_COLS), F32),
                        pltpu.VMEM((tm, D_FF), BF16),
                        pltpu.VMEM((SUBLANES, 2 * D_FF), F32)],
        compiler_params=_params("arbitrary"),
        name="conv_ffn",
    )(x, g_pre, w_up, conv_w, conv_b, w_down, g_post)


def _pad_rows(w, before, total):
    return jnp.pad(w, ((before, total - before - w.shape[0]), (0, 0)))


def kernel(x, mem, mem_norm, norm_mix_pre, norm_mix_post, w_in, mu_shift, pool_w, pool_b,
           pool_scale, w_proj_pool, w_mem_kv, w_proj_mem, w0, w_up_decay, a0, w_up_a, w_up_g,
           k_k, k_a, r_k, ln_x_w, ln_x_b, v0, w_down_v, w_up_v, w_proj_rwkv, gate_b, w_o,
           norm_ffn_pre, norm_ffn_post, w_ffn_up, conv_w, conv_b, w_ffn_down):
    batch, seq, d = x.shape
    depth = w_in.shape[0]
    assert d == D_MODEL and seq % 512 == 0 and mem.shape[1] == MEM_LEN and batch % SCAN_SEQS == 0
    t = batch * seq
    xf = x.reshape(t, d)
    memf = mem.reshape(batch * MEM_LEN, d)
    row = lambda a: a.reshape(1, -1)
    bf = lambda a: a.astype(BF16)

    v_first = None
    for l in range(depth):
        w_pqg = jnp.concatenate([w_in[l][:, :OFF_RWKV], w_in[l][:, OFF_GATE:]], axis=1)
        z = _norm_matmul(xf, row(norm_mix_pre[l]), bf(w_pqg), 512, w_pqg.shape[1] // 2)
        kv = _norm_matmul(memf, row(mem_norm), bf(w_mem_kv[l]), 512, 2 * MEM_WIDTH)
        vres = None
        if l > 0:
            vres = (v_first, row(v0[l - 1]),
                    bf(jnp.pad(w_down_v[l - 1], ((0, 0), (0, LANES - VRES_LORA)))),
                    bf(_pad_rows(w_up_v[l - 1], 0, LANES)))
        r, lw, k, v, kk, b, g = _rwkv_in(
            xf, row(norm_mix_pre[l]), bf(w_in[l][:, OFF_RWKV:OFF_GATE]), row(mu_shift[l]), row(w0[l]),
            bf(_pad_rows(w_up_decay[l], 0, LANES)), row(a0[l]),
            bf(_pad_rows(w_up_a[l], DECAY_LORA, LANES)), bf(w_up_g[l]), row(k_k[l]), row(k_a[l]),
            vres, seq, 512)
        if l == 0:
            v_first = v
        yg = _scan(r, lw, k, v, kk, b, g, row(r_k[l]), row(ln_x_w[l]), row(ln_x_b[l]), batch, seq)
        xf = _mix_out(z, kv, yg, xf, bf(pool_w[l]), row(pool_b[l]), row(pool_scale[l]),
                      bf(w_proj_pool[l]), bf(w_proj_mem[l]), bf(w_proj_rwkv[l]), bf(w_o[l]), gate_b[l],
                      row(norm_mix_post[l]), batch, seq, 512)
        xf = _ffn(xf, row(norm_ffn_pre[l]), bf(w_ffn_up[l]), conv_w[l], row(conv_b[l]),
                  bf(w_ffn_down[l]), row(norm_ffn_post[l]), seq, 512)
    return xf.reshape(batch, seq, d)
```

```python
import functools
import math

import jax
import jax.numpy as jnp
from jax import lax
from jax.experimental import pallas as pl
from jax.experimental.pallas import tpu as pltpu

F32 = jnp.float32
BF16 = jnp.bfloat16

D_MODEL = 1024
MEM_LEN = 256
POOL_WIDTH = 512
POOL_GROUP_DIM = 128
POOL_WINDOWS = (2, 4, 8, 16)
POOL_HALO = 16
MEM_HEADS = 4
MEM_HEAD_DIM = 128
MEM_WIDTH = MEM_HEADS * MEM_HEAD_DIM
RWKV_HEAD_DIM = 64
RWKV_WIDTH = D_MODEL
DECAY_LORA = 64
ICLR_LORA = 64
VRES_LORA = 32
GATE_LORA = 128
LORA_COLS = DECAY_LORA + ICLR_LORA + GATE_LORA
RWKV_COLS = 3 * RWKV_WIDTH + LORA_COLS
N_BRANCH = 3
OFF_Q = POOL_WIDTH
OFF_RWKV = OFF_Q + MEM_WIDTH
OFF_GATE = OFF_RWKV + RWKV_COLS
IN_COLS = OFF_GATE + N_BRANCH * D_MODEL
D_FF = 2816
CONV_WIDTH = 3
NORM_EPS = 1e-6
LNX_EPS = 64e-5
L2_EPS = 1e-12

LANES = 128
SUBLANES = 8
VMEM_LIMIT_BYTES = 56 * 1024 * 1024

SCAN_CHUNK = 64
SCAN_SEQS = 8
HEADS_PER_SLAB = LANES // RWKV_HEAD_DIM
N_SLABS = RWKV_WIDTH // LANES
FFN_COLS = 256
FFN_DOWN_SPLITS = ((0, 4), (4, 8), (8, 11))

ZBLK_PQ, ZBLK_G0, ZBLK_G1, ZBLK_G2 = range(4)


def _params(*sem):
    return pltpu.CompilerParams(dimension_semantics=sem, vmem_limit_bytes=VMEM_LIMIT_BYTES)


def _layer(a, l):
    return pl.BlockSpec((None,) + tuple(a.shape[1:]), lambda *_: (l,) + (0,) * (a.ndim - 1))


def _dot(a, b):
    return jnp.dot(a, b, preferred_element_type=F32)


def _dot_nt(a, b):
    return lax.dot_general(a, b, (((1,), (1,)), ((), ())), preferred_element_type=F32)


def _dot_tn(a, b):
    return lax.dot_general(a, b, (((0,), (0,)), ((), ())), preferred_element_type=F32)


def _sigmoid(x):
    return 1.0 / (1.0 + jnp.exp(-x))


def _rms(x, g):
    ms = jnp.mean(x * x, axis=-1, keepdims=True)
    return x * lax.rsqrt(ms + NORM_EPS) * g


def _split2(x):
    hi = x.astype(BF16)
    lo = (x - hi.astype(F32)).astype(BF16)
    return hi, lo


def _head_sums(xs, seg, two_term=False):
    rows = xs[0].shape[0]
    if not two_term:
        res = _dot(jnp.concatenate([x.astype(BF16) for x in xs], axis=0), seg)
        return [res[i * rows:(i + 1) * rows] for i in range(len(xs))]
    parts = [p for x in xs for p in _split2(x)]
    res = _dot(jnp.concatenate(parts, axis=0), seg)
    return [res[2 * i * rows:(2 * i + 1) * rows] + res[(2 * i + 1) * rows:(2 * i + 2) * rows]
            for i in range(len(xs))]


def _seg_matrix():
    i = lax.broadcasted_iota(jnp.int32, (LANES, LANES), 0) // RWKV_HEAD_DIM
    j = lax.broadcasted_iota(jnp.int32, (LANES, LANES), 1) // RWKV_HEAD_DIM
    return jnp.where(i == j, 1.0, 0.0).astype(BF16)


def _norm_matmul_kernel(x_ref, g_ref, w_ref, o_ref):
    o_ref[...] = _dot(_rms(x_ref[...], g_ref[...]).astype(BF16), w_ref[...]).astype(o_ref.dtype)


def _norm_matmul(x, g, w, l, tm, tn):
    t, d = x.shape
    n = w.shape[2]
    return pl.pallas_call(
        _norm_matmul_kernel,
        grid=(n // tn, t // tm),
        in_specs=[pl.BlockSpec((tm, d), lambda j, i: (i, 0)),
                  _layer(g, l),
                  pl.BlockSpec((None, d, tn), lambda j, i: (l, 0, j))],
        out_specs=pl.BlockSpec((tm, tn), lambda j, i: (i, j)),
        out_shape=jax.ShapeDtypeStruct((t, n), BF16),
        compiler_params=_params("parallel", "parallel"),
        name="norm_matmul",
    )(x, g, w)


def _mix_out_kernel(zpq_ref, halo_ref, zg0_ref, zg1_ref, zg2_ref, kv_ref, yg_ref, x_ref,
                    poolw_ref, poolb_ref, pools_ref, wpp_ref, wpm_ref, wpr_ref, wo_ref, gb_ref,
                    gn_ref, o_ref, *, tm):
    i = pl.program_id(1)
    hd = [slice(h * MEM_HEAD_DIM, (h + 1) * MEM_HEAD_DIM) for h in range(MEM_HEADS)]
    scores = [_dot_nt(zpq_ref[:, POOL_WIDTH + sl.start:POOL_WIDTH + sl.stop].astype(BF16),
                      kv_ref[:, sl].astype(BF16)) * (MEM_HEAD_DIM ** -0.5) for sl in hd]
    y_rwkv = _dot(yg_ref[...], wpr_ref[...])

    p = zpq_ref[:, :POOL_WIDTH].astype(F32)
    prev = jnp.where(i == 0, 0.0, halo_ref[...].astype(F32))
    pe = jnp.concatenate([prev, p], axis=0)
    pos = lax.broadcasted_iota(jnp.int32, (tm, POOL_GROUP_DIM), 0) + i * tm
    mixed = []
    for gi, win in enumerate(POOL_WINDOWS):
        sl = slice(gi * POOL_GROUP_DIM, (gi + 1) * POOL_GROUP_DIM)
        s = pe[:, sl]
        k = 1
        while k < win:
            s = s + pltpu.roll(s, k, axis=0)
            k *= 2
        cnt = jnp.minimum(pos + 1, win).astype(F32)
        pooled = s[POOL_HALO:] / cnt - p[:, sl]
        m = _dot(pooled.astype(BF16), poolw_ref[gi]) + poolb_ref[:, sl]
        mixed.append((m * pools_ref[:, sl]).astype(BF16))
    y_pool = _dot(jnp.concatenate(mixed, axis=1), wpp_ref[...])

    probs = []
    for s in scores:
        e = jnp.exp(s - jnp.max(s, axis=-1, keepdims=True))
        probs.append((e / jnp.sum(e, axis=-1, keepdims=True)).astype(BF16))
    heads = [_dot(probs[h], kv_ref[:, MEM_WIDTH + sl.start:MEM_WIDTH + sl.stop].astype(BF16)).astype(BF16)
             for h, sl in enumerate(hd)]
    y_mem = _dot(jnp.concatenate(heads, axis=1), wpm_ref[...])

    gate = lambda z_ref, n: _sigmoid(z_ref[...].astype(F32) + gb_ref[n:n + 1, :])
    merged = gate(zg0_ref, 0) * y_pool + gate(zg1_ref, 1) * y_rwkv + gate(zg2_ref, 2) * y_mem
    o = _dot(merged.astype(BF16), wo_ref[...])
    o_ref[...] = x_ref[...] + _rms(o, gn_ref[...])


def _mix_out(z, kv, yg, x, params, l, batch, seq, tm):
    t, d = x.shape
    nt = seq // tm
    hb = tm // POOL_HALO
    zblk = lambda blk: pl.BlockSpec((tm, d), lambda b, i: (b * nt + i, blk))
    return pl.pallas_call(
        functools.partial(_mix_out_kernel, tm=tm),
        grid=(batch, nt),
        in_specs=[zblk(ZBLK_PQ),
                  pl.BlockSpec((POOL_HALO, POOL_WIDTH),
                               lambda b, i: (jnp.maximum((b * nt + i) * hb - 1, 0), 0)),
                  zblk(ZBLK_G0), zblk(ZBLK_G1), zblk(ZBLK_G2),
                  pl.BlockSpec((MEM_LEN, 2 * MEM_WIDTH), lambda b, i: (b, 0)),
                  zblk(0), zblk(0)] + [_layer(a, l) for a in params],
        out_specs=zblk(0),
        out_shape=jax.ShapeDtypeStruct((t, d), F32),
        compiler_params=_params("parallel", "arbitrary"),
        name="mix_out",
    )(z, z, z, z, z, kv, yg, x, *params)


def _rwkv_in_kernel(*refs, tm, tiles_per_seq, has_vres):
    (x_ref, gn_ref, w_ref, mu_ref, w0_ref, wdec_ref, a0_ref, wa_ref, wg_ref, kk_ref, ka_ref) = refs[:11]
    refs = refs[11:]
    if has_vres:
        vfirst_ref, v0_ref, wdv_ref, wuv_ref = refs[:4]
        refs = refs[4:]
    r_o, lw_o, k_o, v_o, kk_o, b_o, g_o, carry_ref = refs
    d = RWKV_WIDTH

    xn = _rms(x_ref[...], gn_ref[...]).astype(BF16)
    first = (pl.program_id(0) % tiles_per_seq) == 0

    def project(lo, hi):
        z = _dot(xn, w_ref[:, lo:hi])
        last = jnp.where(first, 0.0, carry_ref[SUBLANES - 1:SUBLANES, lo:hi])
        carry_ref[:, lo:hi] = z[tm - SUBLANES:, :]
        row = lax.broadcasted_iota(jnp.int32, z.shape, 0)
        zp = jnp.where(row == 0, last, pltpu.roll(z, 1, axis=0))
        return z + (zp - z) * mu_ref[:, lo:hi]

    lora = project(3 * d, RWKV_COLS)
    dwa = lora[:, :LANES]
    dg = lora[:, LANES:]
    r_o[...] = project(0, d).astype(r_o.dtype)
    k = project(d, 2 * d)

    u = w0_ref[...] + _dot(jnp.tanh(dwa).astype(BF16), wdec_ref[...])
    lw_o[...] = -math.exp(-0.5) * _sigmoid(u)
    alpha = _sigmoid(a0_ref[...] + _dot(dwa.astype(BF16), wa_ref[...]))
    g_o[...] = _dot(_sigmoid(dg).astype(BF16), wg_ref[...]).astype(g_o.dtype)

    v = project(2 * d, 3 * d)

    seg = _seg_matrix()
    kk = k * kk_ref[...]
    kk2 = kk * kk
    ss = jnp.concatenate(_head_sums([kk2[:, s * LANES:(s + 1) * LANES] for s in range(N_SLABS)], seg), axis=1)
    kk = kk * lax.rsqrt(ss + L2_EPS)
    k_o[...] = (k * (1.0 + (alpha - 1.0) * ka_ref[...])).astype(k_o.dtype)
    kk_o[...] = kk.astype(kk_o.dtype)
    b_o[...] = (kk * alpha).astype(b_o.dtype)

    if has_vres:
        low = _dot(v.astype(BF16), wdv_ref[...])
        vg = _sigmoid(v0_ref[...] + _dot(low.astype(BF16), wuv_ref[...]))
        v = v + (vfirst_ref[...].astype(F32) - v) * vg
    v_o[...] = v.astype(v_o.dtype)


def _rwkv_in(x, params, l, v_first, vres_params, seq, tm):
    t, d = x.shape
    blk = pl.BlockSpec((tm, d), lambda i: (i, 0))
    in_specs = [blk] + [_layer(a, l) for a in params]
    args = [x, *params]
    if v_first is not None:
        in_specs += [blk] + [_layer(a, l - 1) for a in vres_params]
        args += [v_first, *vres_params]
    return pl.pallas_call(
        functools.partial(_rwkv_in_kernel, tm=tm, tiles_per_seq=seq // tm, has_vres=v_first is not None),
        grid=(t // tm,),
        in_specs=in_specs,
        out_specs=[blk] * 7,
        out_shape=[jax.ShapeDtypeStruct((t, d), F32 if n == 1 else BF16) for n in range(7)],
        scratch_shapes=[pltpu.VMEM((SUBLANES, RWKV_COLS), F32)],
        compiler_params=_params("arbitrary"),
        name="rwkv_in",
    )(*args)


def _stack_heads(x, m1):
    return jnp.concatenate([jnp.where(m1, x, 0.0), jnp.where(m1, 0.0, x)], axis=0)


def _scan_kernel(r_ref, lw_ref, k_ref, v_ref, kk_ref, b_ref, g_ref, rk_ref, lnw_ref, lnb_ref,
                 o_ref, h_ref):
    c = SCAN_CHUNK
    c2 = HEADS_PER_SLAB * c

    @pl.when(pl.program_id(1) == 0)
    def _():
        h_ref[...] = jnp.zeros_like(h_ref)

    lane = lax.broadcasted_iota(jnp.int32, (c, LANES), 1)
    m1 = lane < RWKV_HEAD_DIM
    ti = lax.broadcasted_iota(jnp.int32, (c, LANES), 0)
    tj = lane % RWKV_HEAD_DIM
    strict = ti > tj
    incl = ti >= tj
    eye = jnp.where(ti == tj, 1.0, 0.0)
    seg = _seg_matrix()
    inv_n = 1.0 / RWKV_HEAD_DIM
    ltri = jnp.where(lax.broadcasted_iota(jnp.int32, (c, c), 0)
                     >= lax.broadcasted_iota(jnp.int32, (c, c), 1), 1.0, 0.0).astype(BF16)

    units = [(j, slice(s * LANES, (s + 1) * LANES)) for j in range(SCAN_SEQS) for s in range(N_SLABS)]
    nu = range(len(units))

    lws, cums = [], []
    for j in range(SCAN_SEQS):
        lw = lw_ref[j]
        hi = lw.astype(BF16)
        r1 = lw - hi.astype(F32)
        mid = r1.astype(BF16)
        lo = (r1 - mid.astype(F32)).astype(BF16)
        lws.append(lw)
        cums.append(_dot(ltri, hi) + _dot(ltri, mid) + _dot(ltri, lo))

    ar, d_ak, d_r, tinv, apow, bk, decay_c = [], [], [], [], [], [], []
    for j, sl in units:
        cum_s = cums[j][:, sl]
        last = cum_s[c - 1:c, :]
        p_inv = jnp.exp(-cum_s)
        p_end = jnp.exp(last - cum_s)
        kx = k_ref[j, :, sl].astype(F32)
        bb = b_ref[j, :, sl].astype(F32)
        a_t = -kk_ref[j, :, sl].astype(F32) * jnp.exp(cum_s - lws[j][:, sl])
        r_t = r_ref[j, :, sl].astype(F32) * jnp.exp(cum_s)
        ar_s = jnp.concatenate([a_t, r_t], axis=0).astype(BF16)
        xr = jnp.concatenate([_stack_heads((bb * p_inv).astype(BF16), m1),
                              _stack_heads((kx * p_inv).astype(BF16), m1)], axis=0)
        gm = _dot_nt(ar_s, xr)
        d_ab = jnp.where(strict, gm[:c, :c2], 0.0)
        d_ak.append(jnp.where(strict, gm[:c, c2:], 0.0).astype(BF16))
        d_r.append(jnp.concatenate([jnp.where(incl, gm[c:, :c2], 0.0),
                                    jnp.where(incl, gm[c:, c2:], 0.0)], axis=1).astype(BF16))
        tinv.append(eye + d_ab)
        apow.append(d_ab.astype(BF16))
        ar.append(ar_s)
        bk.append(jnp.concatenate([_stack_heads((bb * p_end).astype(BF16), m1),
                                   _stack_heads((kx * p_end).astype(BF16), m1)], axis=0))
        decay_c.append(jnp.exp(last))

    for i in nu:
        apow[i] = _dot(apow[i], _stack_heads(apow[i], m1)).astype(BF16)
    n = 4
    while n < c:
        for i in nu:
            both = _dot(jnp.concatenate([tinv[i].astype(BF16), apow[i]], axis=0),
                        _stack_heads(apow[i], m1))
            tinv[i] = tinv[i] + both[:c]
            apow[i] = both[c:].astype(BF16)
        n *= 2
    for i in nu:
        tinv[i] = (tinv[i] + _dot(tinv[i].astype(BF16), _stack_heads(apow[i], m1))).astype(BF16)

    hs = [h_ref[i] for i in nu]
    xh = [_dot_nt(ar[i], hs[i].astype(BF16)) for i in nu]
    v_st = [_stack_heads(v_ref[j, :, sl], m1).astype(BF16) for j, sl in units]
    z = [xh[i][:c] + _dot(d_ak[i], v_st[i]) for i in nu]
    u = [_dot(tinv[i], _stack_heads(z[i].astype(BF16), m1)) for i in nu]
    uv = [jnp.concatenate([_stack_heads(u[i].astype(BF16), m1), v_st[i]], axis=0) for i in nu]
    y = [xh[i][c:] + _dot(d_r[i], uv[i]) for i in nu]
    for i in nu:
        h_ref[i] = hs[i] * decay_c[i] + _dot_tn(uv[i], bk[i])

    bonus = _head_sums([r_ref[j, :, sl].astype(F32) * k_ref[j, :, sl].astype(F32) * rk_ref[:, sl]
                        for j, sl in units], seg)
    mu = _head_sums(y, seg, two_term=True)
    dlt = [y[i] - mu[i] * inv_n for i in nu]
    var = _head_sums([dlt[i] * dlt[i] for i in nu], seg)
    for i, (j, sl) in enumerate(units):
        yn = dlt[i] * lax.rsqrt(var[i] * inv_n + LNX_EPS) * lnw_ref[:, sl] + lnb_ref[:, sl]
        o_ref[j, :, sl] = ((yn + bonus[i] * v_ref[j, :, sl].astype(F32))
                           * g_ref[j, :, sl].astype(F32)).astype(BF16)


def _scan(r, lw, k, v, kk, b, g, params, l, batch, seq):
    t, d = r.shape
    c = SCAN_CHUNK
    as_seqs = lambda a: a.reshape(batch, seq, d)
    blk = pl.BlockSpec((SCAN_SEQS, c, d), lambda bi, ci: (bi, ci, 0))
    out = pl.pallas_call(
        _scan_kernel,
        grid=(batch // SCAN_SEQS, seq // c),
        in_specs=[blk] * 7 + [_layer(a, l) for a in params],
        out_specs=blk,
        out_shape=jax.ShapeDtypeStruct((batch, seq, d), BF16),
        scratch_shapes=[pltpu.VMEM((SCAN_SEQS * N_SLABS, LANES, LANES), F32)],
        compiler_params=_params("parallel", "arbitrary"),
        name="rwkv_scan",
    )(*(as_seqs(a) for a in (r, lw, k, v, kk, b, g)), *params)
    return out.reshape(t, d)


GELU_C0 = math.sqrt(2.0 / math.pi)
GELU_C1 = GELU_C0 * 0.044715


def _gelu_tanh(x):
    hx = 0.5 * x
    return hx + hx * jnp.tanh(x * (GELU_C0 + GELU_C1 * (x * x)))


def _ffn_kernel(x_ref, gpre_ref, wup_ref, cw_ref, cb_ref, wd_ref, gpost_ref, o_ref,
                ubuf_ref, f_ref, carry_ref, *, tm, tiles_per_seq):
    first = (pl.program_id(0) % tiles_per_seq) == 0
    fc = FFN_COLS
    nf = D_FF // fc
    hn = _rms(x_ref[...], gpre_ref[...]).astype(BF16)
    for j in range(nf):
        halves = []
        for half in range(2):
            cols = slice(half * D_FF + j * fc, half * D_FF + (j + 1) * fc)
            buf = ubuf_ref.at[2 * (j % 2) + half]
            u = _dot(hn, wup_ref[:, cols])
            buf[0:SUBLANES, :] = jnp.where(first, 0.0, carry_ref[:, cols])
            buf[SUBLANES:, :] = u
            carry_ref[:, cols] = u[tm - SUBLANES:, :]
            out = cb_ref[:, cols] + cw_ref[0:1, cols] * buf[SUBLANES - 2:SUBLANES - 2 + tm, :]
            out = out + cw_ref[1:2, cols] * buf[SUBLANES - 1:SUBLANES - 1 + tm, :]
            halves.append(out + cw_ref[2:3, cols] * u)
        f_ref[:, j * fc:(j + 1) * fc] = (_gelu_tanh(halves[0]) * halves[1]).astype(BF16)
    acc = None
    for lo, hi in FFN_DOWN_SPLITS:
        part = _dot(f_ref[:, lo * fc:hi * fc], wd_ref[lo * fc:hi * fc, :])
        acc = part if acc is None else part + acc
    o_ref[...] = x_ref[...] + _rms(acc, gpost_ref[...])


def _ffn(x, params, l, seq, tm):
    t, d = x.shape
    xblk = pl.BlockSpec((tm, d), lambda i: (i, 0))
    return pl.pallas_call(
        functools.partial(_ffn_kernel, tm=tm, tiles_per_seq=seq // tm),
        grid=(t // tm,),
        in_specs=[xblk] + [_layer(a, l) for a in params],
        out_specs=xblk,
        out_shape=jax.ShapeDtypeStruct((t, d), F32),
        scratch_shapes=[pltpu.VMEM((4, SUBLANES + tm, FFN_COLS), F32),
                        pltpu.VMEM((tm, D_FF), BF16),
                        pltpu.VMEM((SUBLANES, 2 * D_FF), F32)],
        compiler_params=_params("arbitrary"),
        name="conv_ffn",
    )(x, *params)


def kernel(x, mem, mem_norm, norm_mix_pre, norm_mix_post, w_in, mu_shift, pool_w, pool_b,
           pool_scale, w_proj_pool, w_mem_kv, w_proj_mem, w0, w_up_decay, a0, w_up_a, w_up_g,
           k_k, k_a, r_k, ln_x_w, ln_x_b, v0, w_down_v, w_up_v, w_proj_rwkv, gate_b, w_o,
           norm_ffn_pre, norm_ffn_post, w_ffn_up, conv_w, conv_b, w_ffn_down):
    batch, seq, d = x.shape
    depth = w_in.shape[0]
    assert d == D_MODEL and seq % 512 == 0 and mem.shape[1] == MEM_LEN and batch % SCAN_SEQS == 0
    t = batch * seq
    xf = x.reshape(t, d)
    memf = mem.reshape(batch * MEM_LEN, d)
    vec = lambda a: a.reshape(a.shape[0], 1, math.prod(a.shape[1:]))
    bf = lambda a: a.astype(BF16)
    pad = lambda a, rows, cols: jnp.pad(a, ((0, 0), rows, cols))
    w_pqg = bf(jnp.concatenate([w_in[:, :, :OFF_RWKV], w_in[:, :, OFF_GATE:]], axis=2))
    g_mix_pre = vec(norm_mix_pre)
    g_mem = jnp.broadcast_to(mem_norm.reshape(1, 1, d), (depth, 1, d))
    w_kv = bf(w_mem_kv)
    rwkv_in_params = [g_mix_pre, bf(w_in[:, :, OFF_RWKV:OFF_GATE]), vec(mu_shift), vec(w0),
                      bf(pad(w_up_decay, (0, LANES - DECAY_LORA), (0, 0))), vec(a0),
                      bf(pad(w_up_a, (DECAY_LORA, 0), (0, 0))), bf(w_up_g), vec(k_k), vec(k_a)]
    vres_params = [vec(v0), bf(pad(w_down_v, (0, 0), (0, LANES - VRES_LORA))),
                   bf(pad(w_up_v, (0, LANES - VRES_LORA), (0, 0)))]
    scan_params = [vec(r_k), vec(ln_x_w), vec(ln_x_b)]
    mix_params = [bf(pool_w), vec(pool_b), vec(pool_scale), bf(w_proj_pool), bf(w_proj_mem),
                  bf(w_proj_rwkv), bf(w_o), gate_b, vec(norm_mix_post)]
    ffn_params = [vec(norm_ffn_pre), bf(w_ffn_up), conv_w, vec(conv_b), bf(w_ffn_down), vec(norm_ffn_post)]

    v_first = None
    for l in range(depth):
        z = _norm_matmul(xf, g_mix_pre, w_pqg, l, 1024, w_pqg.shape[2] // 2)
        kv = _norm_matmul(memf, g_mem, w_kv, l, 512, 2 * MEM_WIDTH)
        r, lw, k, v, kk, b, g = _rwkv_in(xf, rwkv_in_params, l, v_first, vres_params, seq, 512)
        if l == 0:
            v_first = v
        yg = _scan(r, lw, k, v, kk, b, g, scan_params, l, batch, seq)
        xf = _mix_out(z, kv, yg, xf, mix_params, l, batch, seq, 512)
        xf = _ffn(xf, ffn_params, l, seq, 512)
    return xf.reshape(batch, seq, d)
```

```python
import functools
import math

import jax
import jax.numpy as jnp
from jax import lax
from jax.experimental import pallas as pl
from jax.experimental.pallas import tpu as pltpu

F32 = jnp.float32
BF16 = jnp.bfloat16

D_MODEL = 1024
MEM_LEN = 256
POOL_WIDTH = 512
POOL_GROUP_DIM = 128
POOL_WINDOWS = (2, 4, 8, 16)
POOL_HALO = 16
MEM_HEADS = 4
MEM_HEAD_DIM = 128
MEM_WIDTH = MEM_HEADS * MEM_HEAD_DIM
RWKV_HEAD_DIM = 64
RWKV_WIDTH = D_MODEL
DECAY_LORA = 64
ICLR_LORA = 64
VRES_LORA = 32
GATE_LORA = 128
LORA_COLS = DECAY_LORA + ICLR_LORA + GATE_LORA
RWKV_COLS = 3 * RWKV_WIDTH + LORA_COLS
N_BRANCH = 3
OFF_Q = POOL_WIDTH
OFF_RWKV = OFF_Q + MEM_WIDTH
OFF_GATE = OFF_RWKV + RWKV_COLS
IN_COLS = OFF_GATE + N_BRANCH * D_MODEL
D_FF = 2816
CONV_WIDTH = 3
NORM_EPS = 1e-6
LNX_EPS = 64e-5
L2_EPS = 1e-12

LANES = 128
SUBLANES = 8
VMEM_LIMIT_BYTES = 56 * 1024 * 1024

PROJ_ROWS = 1024
SEQ_ROWS = 512
SCAN_CHUNK = 64
SCAN_SEQS = 8
HEADS_PER_SLAB = LANES // RWKV_HEAD_DIM
N_SLABS = RWKV_WIDTH // LANES
FFN_COLS = 256
FFN_DOWN_SPLITS = ((0, 4), (4, 8), (8, 11))

ZBLK_PQ, ZBLK_G0, ZBLK_G1, ZBLK_G2 = range(4)


def _params(*sem):
    return pltpu.CompilerParams(dimension_semantics=sem, vmem_limit_bytes=VMEM_LIMIT_BYTES)


def _layer(a, l):
    return pl.BlockSpec((None,) + tuple(a.shape[1:]), lambda *_: (l,) + (0,) * (a.ndim - 1))


def _dot(a, b):
    return jnp.dot(a, b, preferred_element_type=F32)


def _dot_nt(a, b):
    return lax.dot_general(a, b, (((1,), (1,)), ((), ())), preferred_element_type=F32)


def _dot_tn(a, b):
    return lax.dot_general(a, b, (((0,), (0,)), ((), ())), preferred_element_type=F32)


def _sigmoid(x):
    return 1.0 / (1.0 + jnp.exp(-x))


def _rms(x, g):
    ms = jnp.mean(x * x, axis=-1, keepdims=True)
    return x * lax.rsqrt(ms + NORM_EPS) * g


def _split2(x):
    hi = x.astype(BF16)
    lo = (x - hi.astype(F32)).astype(BF16)
    return hi, lo


def _head_sums(xs, seg, two_term=False):
    rows = xs[0].shape[0]
    if not two_term:
        res = _dot(jnp.concatenate([x.astype(BF16) for x in xs], axis=0), seg)
        return [res[i * rows:(i + 1) * rows] for i in range(len(xs))]
    parts = [p for x in xs for p in _split2(x)]
    res = _dot(jnp.concatenate(parts, axis=0), seg)
    return [res[2 * i * rows:(2 * i + 1) * rows] + res[(2 * i + 1) * rows:(2 * i + 2) * rows]
            for i in range(len(xs))]


def _seg_matrix():
    i = lax.broadcasted_iota(jnp.int32, (LANES, LANES), 0) // RWKV_HEAD_DIM
    j = lax.broadcasted_iota(jnp.int32, (LANES, LANES), 1) // RWKV_HEAD_DIM
    return jnp.where(i == j, 1.0, 0.0).astype(BF16)


def _norm_matmul_kernel(x_ref, g_ref, w_ref, o_ref):
    o_ref[...] = _dot(_rms(x_ref[...], g_ref[...]).astype(BF16), w_ref[...]).astype(o_ref.dtype)


def _norm_matmul(x, g, w, l, tm, tn):
    t, d = x.shape
    n = w.shape[2]
    return pl.pallas_call(
        _norm_matmul_kernel,
        grid=(n // tn, t // tm),
        in_specs=[pl.BlockSpec((tm, d), lambda j, i: (i, 0)),
                  _layer(g, l),
                  pl.BlockSpec((None, d, tn), lambda j, i: (l, 0, j))],
        out_specs=pl.BlockSpec((tm, tn), lambda j, i: (i, j)),
        out_shape=jax.ShapeDtypeStruct((t, n), BF16),
        compiler_params=_params("parallel", "parallel"),
        name="norm_matmul",
    )(x, g, w)


def _mix_out_kernel(zpq_ref, halo_ref, zg0_ref, zg1_ref, zg2_ref, kv_ref, yg_ref, x_ref,
                    poolw_ref, poolb_ref, pools_ref, wpp_ref, wpm_ref, wpr_ref, wo_ref, gb_ref,
                    gn_ref, o_ref, *, tm):
    i = pl.program_id(1)
    hd = [slice(h * MEM_HEAD_DIM, (h + 1) * MEM_HEAD_DIM) for h in range(MEM_HEADS)]
    scores = [_dot_nt(zpq_ref[:, POOL_WIDTH + sl.start:POOL_WIDTH + sl.stop].astype(BF16),
                      kv_ref[:, sl].astype(BF16)) * (MEM_HEAD_DIM ** -0.5) for sl in hd]
    y_rwkv = _dot(yg_ref[...], wpr_ref[...])

    p = zpq_ref[:, :POOL_WIDTH].astype(F32)
    prev = jnp.where(i == 0, 0.0, halo_ref[...].astype(F32))
    pe = jnp.concatenate([prev, p], axis=0)
    pos = lax.broadcasted_iota(jnp.int32, (tm, POOL_GROUP_DIM), 0) + i * tm
    mixed = []
    for gi, win in enumerate(POOL_WINDOWS):
        sl = slice(gi * POOL_GROUP_DIM, (gi + 1) * POOL_GROUP_DIM)
        s = pe[:, sl]
        k = 1
        while k < win:
            s = s + pltpu.roll(s, k, axis=0)
            k *= 2
        cnt = jnp.minimum(pos + 1, win).astype(F32)
        pooled = s[POOL_HALO:] / cnt - p[:, sl]
        m = _dot(pooled.astype(BF16), poolw_ref[gi]) + poolb_ref[:, sl]
        mixed.append((m * pools_ref[:, sl]).astype(BF16))
    y_pool = _dot(jnp.concatenate(mixed, axis=1), wpp_ref[...])

    probs = []
    for s in scores:
        e = jnp.exp(s - jnp.max(s, axis=-1, keepdims=True))
        probs.append((e / jnp.sum(e, axis=-1, keepdims=True)).astype(BF16))
    heads = [_dot(probs[h], kv_ref[:, MEM_WIDTH + sl.start:MEM_WIDTH + sl.stop].astype(BF16)).astype(BF16)
             for h, sl in enumerate(hd)]
    y_mem = _dot(jnp.concatenate(heads, axis=1), wpm_ref[...])

    gate = lambda z_ref, n: _sigmoid(z_ref[...].astype(F32) + gb_ref[n:n + 1, :])
    merged = gate(zg0_ref, 0) * y_pool + gate(zg1_ref, 1) * y_rwkv + gate(zg2_ref, 2) * y_mem
    o = _dot(merged.astype(BF16), wo_ref[...])
    o_ref[...] = x_ref[...] + _rms(o, gn_ref[...])


def _mix_out(z, kv, yg, x, params, l, batch, seq, tm):
    t, d = x.shape
    nt = seq // tm
    hb = tm // POOL_HALO
    zblk = lambda blk: pl.BlockSpec((tm, d), lambda b, i: (b * nt + i, blk))
    return pl.pallas_call(
        functools.partial(_mix_out_kernel, tm=tm),
        grid=(batch, nt),
        in_specs=[zblk(ZBLK_PQ),
                  pl.BlockSpec((POOL_HALO, POOL_WIDTH),
                               lambda b, i: (jnp.maximum((b * nt + i) * hb - 1, 0), 0)),
                  zblk(ZBLK_G0), zblk(ZBLK_G1), zblk(ZBLK_G2),
                  pl.BlockSpec((MEM_LEN, 2 * MEM_WIDTH), lambda b, i: (b, 0)),
                  zblk(0), zblk(0)] + [_layer(a, l) for a in params],
        out_specs=zblk(0),
        out_shape=jax.ShapeDtypeStruct((t, d), F32),
        compiler_params=_params("parallel", "arbitrary"),
        name="mix_out",
    )(z, z, z, z, z, kv, yg, x, *params)


def _rwkv_in_kernel(*refs, tm, tiles_per_seq, has_vres):
    (x_ref, gn_ref, w_ref, mu_ref, w0_ref, wdec_ref, a0_ref, wa_ref, wg_ref, kk_ref, ka_ref) = refs[:11]
    refs = refs[11:]
    if has_vres:
        vfirst_ref, v0_ref, wdv_ref, wuv_ref = refs[:4]
        refs = refs[4:]
    r_o, lw_o, k_o, v_o, kk_o, b_o, g_o, carry_ref = refs
    d = RWKV_WIDTH

    xn = _rms(x_ref[...], gn_ref[...]).astype(BF16)
    first = (pl.program_id(0) % tiles_per_seq) == 0

    def project(lo, hi):
        z = _dot(xn, w_ref[:, lo:hi])
        last = jnp.where(first, 0.0, carry_ref[SUBLANES - 1:SUBLANES, lo:hi])
        carry_ref[:, lo:hi] = z[tm - SUBLANES:, :]
        row = lax.broadcasted_iota(jnp.int32, z.shape, 0)
        zp = jnp.where(row == 0, last, pltpu.roll(z, 1, axis=0))
        return z + (zp - z) * mu_ref[:, lo:hi]

    lora = project(3 * d, RWKV_COLS)
    dwa = lora[:, :LANES]
    dg = lora[:, LANES:]
    r_o[...] = project(0, d).astype(r_o.dtype)
    k = project(d, 2 * d)

    u = w0_ref[...] + _dot(jnp.tanh(dwa).astype(BF16), wdec_ref[...])
    lw_o[...] = -math.exp(-0.5) * _sigmoid(u)
    alpha = _sigmoid(a0_ref[...] + _dot(dwa.astype(BF16), wa_ref[...]))
    g_o[...] = _dot(_sigmoid(dg).astype(BF16), wg_ref[...]).astype(g_o.dtype)

    v = project(2 * d, 3 * d)

    seg = _seg_matrix()
    kk = k * kk_ref[...]
    kk2 = kk * kk
    ss = jnp.concatenate(_head_sums([kk2[:, s * LANES:(s + 1) * LANES] for s in range(N_SLABS)], seg), axis=1)
    kk = kk * lax.rsqrt(ss + L2_EPS)
    k_o[...] = (k * (1.0 + (alpha - 1.0) * ka_ref[...])).astype(k_o.dtype)
    kk_o[...] = kk.astype(kk_o.dtype)
    b_o[...] = (kk * alpha).astype(b_o.dtype)

    if has_vres:
        low = _dot(v.astype(BF16), wdv_ref[...])
        vg = _sigmoid(v0_ref[...] + _dot(low.astype(BF16), wuv_ref[...]))
        v = v + (vfirst_ref[...].astype(F32) - v) * vg
    v_o[...] = v.astype(v_o.dtype)


def _rwkv_in(x, params, l, v_first, vres_params, seq, tm):
    t, d = x.shape
    blk = pl.BlockSpec((tm, d), lambda i: (i, 0))
    in_specs = [blk] + [_layer(a, l) for a in params]
    args = [x, *params]
    if v_first is not None:
        in_specs += [blk] + [_layer(a, l - 1) for a in vres_params]
        args += [v_first, *vres_params]
    return pl.pallas_call(
        functools.partial(_rwkv_in_kernel, tm=tm, tiles_per_seq=seq // tm, has_vres=v_first is not None),
        grid=(t // tm,),
        in_specs=in_specs,
        out_specs=[blk] * 7,
        out_shape=[jax.ShapeDtypeStruct((t, d), F32 if n == 1 else BF16) for n in range(7)],
        scratch_shapes=[pltpu.VMEM((SUBLANES, RWKV_COLS), F32)],
        compiler_params=_params("arbitrary"),
        name="rwkv_in",
    )(*args)


def _stack_heads(x, m1):
    return jnp.concatenate([jnp.where(m1, x, 0.0), jnp.where(m1, 0.0, x)], axis=0)


def _scan_kernel(r_ref, lw_ref, k_ref, v_ref, kk_ref, b_ref, g_ref, rk_ref, lnw_ref, lnb_ref,
                 o_ref, h_ref):
    c = SCAN_CHUNK
    c2 = HEADS_PER_SLAB * c

    @pl.when(pl.program_id(1) == 0)
    def _():
        h_ref[...] = jnp.zeros_like(h_ref)

    lane = lax.broadcasted_iota(jnp.int32, (c, LANES), 1)
    m1 = lane < RWKV_HEAD_DIM
    ti = lax.broadcasted_iota(jnp.int32, (c, LANES), 0)
    tj = lane % RWKV_HEAD_DIM
    strict = ti > tj
    incl = ti >= tj
    eye = jnp.where(ti == tj, 1.0, 0.0)
    seg = _seg_matrix()
    inv_n = 1.0 / RWKV_HEAD_DIM
    ltri = jnp.where(lax.broadcasted_iota(jnp.int32, (c, c), 0)
                     >= lax.broadcasted_iota(jnp.int32, (c, c), 1), 1.0, 0.0).astype(BF16)

    units = [(j, slice(s * LANES, (s + 1) * LANES)) for j in range(SCAN_SEQS) for s in range(N_SLABS)]
    nu = range(len(units))

    ltri3 = jnp.concatenate([ltri] * 3, axis=1)
    lws, cums = [], []
    for j in range(SCAN_SEQS):
        lw = lw_ref[j]
        hi = lw.astype(BF16)
        r1 = lw - hi.astype(F32)
        mid = r1.astype(BF16)
        lo = (r1 - mid.astype(F32)).astype(BF16)
        lws.append(lw)
        cums.append(_dot(ltri3, jnp.concatenate([hi, mid, lo], axis=0)))

    ar, d_ak, d_r, tinv, apow, bk, decay_c = [], [], [], [], [], [], []
    for j, sl in units:
        cum_s = cums[j][:, sl]
        last = cum_s[c - 1:c, :]
        p_inv = jnp.exp(-cum_s)
        p_end = jnp.exp(last - cum_s)
        kx = k_ref[j, :, sl].astype(F32)
        bb = b_ref[j, :, sl].astype(F32)
        a_t = -kk_ref[j, :, sl].astype(F32) * jnp.exp(cum_s - lws[j][:, sl])
        r_t = r_ref[j, :, sl].astype(F32) * jnp.exp(cum_s)
        ar_s = jnp.concatenate([a_t, r_t], axis=0).astype(BF16)
        xr = jnp.concatenate([_stack_heads((bb * p_inv).astype(BF16), m1),
                              _stack_heads((kx * p_inv).astype(BF16), m1)], axis=0)
        gm = _dot_nt(ar_s, xr)
        d_ab = jnp.where(strict, gm[:c, :c2], 0.0)
        d_ak.append(jnp.where(strict, gm[:c, c2:], 0.0).astype(BF16))
        d_r.append(jnp.concatenate([jnp.where(incl, gm[c:, :c2], 0.0),
                                    jnp.where(incl, gm[c:, c2:], 0.0)], axis=1).astype(BF16))
        tinv.append(eye + d_ab)
        apow.append(d_ab.astype(BF16))
        ar.append(ar_s)
        bk.append(jnp.concatenate([_stack_heads((bb * p_end).astype(BF16), m1),
                                   _stack_heads((kx * p_end).astype(BF16), m1)], axis=0))
        decay_c.append(jnp.exp(last))

    for i in nu:
        apow[i] = _dot(apow[i], _stack_heads(apow[i], m1)).astype(BF16)
    n = 4
    while n < c:
        for i in nu:
            both = _dot(jnp.concatenate([tinv[i].astype(BF16), apow[i]], axis=0),
                        _stack_heads(apow[i], m1))
            tinv[i] = tinv[i] + both[:c]
            apow[i] = both[c:].astype(BF16)
        n *= 2
    for i in nu:
        tinv[i] = (tinv[i] + _dot(tinv[i].astype(BF16), _stack_heads(apow[i], m1))).astype(BF16)

    hs = [h_ref[i] for i in nu]
    xh = [_dot_nt(ar[i], hs[i].astype(BF16)) for i in nu]
    v_st = [_stack_heads(v_ref[j, :, sl], m1).astype(BF16) for j, sl in units]
    z = [xh[i][:c] + _dot(d_ak[i], v_st[i]) for i in nu]
    u = [_dot(tinv[i], _stack_heads(z[i].astype(BF16), m1)) for i in nu]
    uv = [jnp.concatenate([_stack_heads(u[i].astype(BF16), m1), v_st[i]], axis=0) for i in nu]
    y = [xh[i][c:] + _dot(d_r[i], uv[i]) for i in nu]
    for i in nu:
        h_ref[i] = hs[i] * decay_c[i] + _dot_tn(uv[i], bk[i])

    bonus = _head_sums([r_ref[j, :, sl].astype(F32) * k_ref[j, :, sl].astype(F32) * rk_ref[:, sl]
                        for j, sl in units], seg)
    mu = _head_sums(y, seg, two_term=True)
    dlt = [y[i] - mu[i] * inv_n for i in nu]
    var = _head_sums([dlt[i] * dlt[i] for i in nu], seg)
    for i, (j, sl) in enumerate(units):
        yn = dlt[i] * lax.rsqrt(var[i] * inv_n + LNX_EPS) * lnw_ref[:, sl] + lnb_ref[:, sl]
        o_ref[j, :, sl] = ((yn + bonus[i] * v_ref[j, :, sl].astype(F32))
                           * g_ref[j, :, sl].astype(F32)).astype(BF16)


def _scan(r, lw, k, v, kk, b, g, params, l, batch, seq):
    t, d = r.shape
    c = SCAN_CHUNK
    as_seqs = lambda a: a.reshape(batch, seq, d)
    blk = pl.BlockSpec((SCAN_SEQS, c, d), lambda bi, ci: (bi, ci, 0))
    out = pl.pallas_call(
        _scan_kernel,
        grid=(batch // SCAN_SEQS, seq // c),
        in_specs=[blk] * 7 + [_layer(a, l) for a in params],
        out_specs=blk,
        out_shape=jax.ShapeDtypeStruct((batch, seq, d), BF16),
        scratch_shapes=[pltpu.VMEM((SCAN_SEQS * N_SLABS, LANES, LANES), F32)],
        compiler_params=_params("parallel", "arbitrary"),
        name="rwkv_scan",
    )(*(as_seqs(a) for a in (r, lw, k, v, kk, b, g)), *params)
    return out.reshape(t, d)


GELU_C0 = math.sqrt(2.0 / math.pi)
GELU_C1 = GELU_C0 * 0.044715


def _gelu_tanh(x):
    hx = 0.5 * x
    return hx + hx * jnp.tanh(x * (GELU_C0 + GELU_C1 * (x * x)))


def _ffn_kernel(x_ref, gpre_ref, wup_ref, cw_ref, cb_ref, wd_ref, gpost_ref, o_ref,
                ubuf_ref, f_ref, carry_ref, *, tm, tiles_per_seq):
    first = (pl.program_id(0) % tiles_per_seq) == 0
    fc = FFN_COLS
    nf = D_FF // fc
    hn = _rms(x_ref[...], gpre_ref[...]).astype(BF16)
    for j in range(nf):
        halves = []
        for half in range(2):
            cols = slice(half * D_FF + j * fc, half * D_FF + (j + 1) * fc)
            buf = ubuf_ref.at[2 * (j % 2) + half]
            u = _dot(hn, wup_ref[:, cols])
            buf[0:SUBLANES, :] = jnp.where(first, 0.0, carry_ref[:, cols])
            buf[SUBLANES:, :] = u
            carry_ref[:, cols] = u[tm - SUBLANES:, :]
            out = cb_ref[:, cols] + cw_ref[0:1, cols] * buf[SUBLANES - 2:SUBLANES - 2 + tm, :]
            out = out + cw_ref[1:2, cols] * buf[SUBLANES - 1:SUBLANES - 1 + tm, :]
            halves.append(out + cw_ref[2:3, cols] * u)
        f_ref[:, j * fc:(j + 1) * fc] = (_gelu_tanh(halves[0]) * halves[1]).astype(BF16)
    acc = None
    for lo, hi in FFN_DOWN_SPLITS:
        part = _dot(f_ref[:, lo * fc:hi * fc], wd_ref[lo * fc:hi * fc, :])
        acc = part if acc is None else part + acc
    o_ref[...] = x_ref[...] + _rms(acc, gpost_ref[...])


def _ffn(x, params, l, seq, tm):
    t, d = x.shape
    xblk = pl.BlockSpec((tm, d), lambda i: (i, 0))
    return pl.pallas_call(
        functools.partial(_ffn_kernel, tm=tm, tiles_per_seq=seq // tm),
        grid=(t // tm,),
        in_specs=[xblk] + [_layer(a, l) for a in params],
        out_specs=xblk,
        out_shape=jax.ShapeDtypeStruct((t, d), F32),
        scratch_shapes=[pltpu.VMEM((4, SUBLANES + tm, FFN_COLS), F32),
                        pltpu.VMEM((tm, D_FF), BF16),
                        pltpu.VMEM((SUBLANES, 2 * D_FF), F32)],
        compiler_params=_params("arbitrary"),
        name="conv_ffn",
    )(x, *params)


def kernel(x, mem, mem_norm, norm_mix_pre, norm_mix_post, w_in, mu_shift, pool_w, pool_b,
           pool_scale, w_proj_pool, w_mem_kv, w_proj_mem, w0, w_up_decay, a0, w_up_a, w_up_g,
           k_k, k_a, r_k, ln_x_w, ln_x_b, v0, w_down_v, w_up_v, w_proj_rwkv, gate_b, w_o,
           norm_ffn_pre, norm_ffn_post, w_ffn_up, conv_w, conv_b, w_ffn_down):
    batch, seq, d = x.shape
    depth = w_in.shape[0]
    t = batch * seq
    assert d == D_MODEL and mem.shape[1] == MEM_LEN and batch % SCAN_SEQS == 0
    assert seq % SEQ_ROWS == 0 and t % PROJ_ROWS == 0 and (batch * MEM_LEN) % SEQ_ROWS == 0
    xf = x.reshape(t, d)
    memf = mem.reshape(batch * MEM_LEN, d)
    vec = lambda a: a.reshape(a.shape[0], 1, math.prod(a.shape[1:]))
    bf = lambda a: a.astype(BF16)
    pad = lambda a, rows, cols: jnp.pad(a, ((0, 0), rows, cols))
    w_pqg = bf(jnp.concatenate([w_in[:, :, :OFF_RWKV], w_in[:, :, OFF_GATE:]], axis=2))
    g_mix_pre = vec(norm_mix_pre)
    g_mem = jnp.broadcast_to(mem_norm.reshape(1, 1, d), (depth, 1, d))
    w_kv = bf(w_mem_kv)
    rwkv_in_params = [g_mix_pre, bf(w_in[:, :, OFF_RWKV:OFF_GATE]), vec(mu_shift), vec(w0),
                      bf(pad(w_up_decay, (0, LANES - DECAY_LORA), (0, 0))), vec(a0),
                      bf(pad(w_up_a, (DECAY_LORA, 0), (0, 0))), bf(w_up_g), vec(k_k), vec(k_a)]
    vres_params = [vec(v0), bf(pad(w_down_v, (0, 0), (0, LANES - VRES_LORA))),
                   bf(pad(w_up_v, (0, LANES - VRES_LORA), (0, 0)))]
    scan_params = [vec(r_k), vec(ln_x_w), vec(ln_x_b)]
    mix_params = [bf(pool_w), vec(pool_b), vec(pool_scale), bf(w_proj_pool), bf(w_proj_mem),
                  bf(w_proj_rwkv), bf(w_o), gate_b, vec(norm_mix_post)]
    ffn_params = [vec(norm_ffn_pre), bf(w_ffn_up), conv_w, vec(conv_b), bf(w_ffn_down), vec(norm_ffn_post)]

    v_first = None
    for l in range(depth):
        z = _norm_matmul(xf, g_mix_pre, w_pqg, l, PROJ_ROWS, w_pqg.shape[2] // 2)
        kv = _norm_matmul(memf, g_mem, w_kv, l, SEQ_ROWS, 2 * MEM_WIDTH)
        r, lw, k, v, kk, b, g = _rwkv_in(xf, rwkv_in_params, l, v_first, vres_params, seq, SEQ_ROWS)
        if l == 0:
            v_first = v
        yg = _scan(r, lw, k, v, kk, b, g, scan_params, l, batch, seq)
        xf = _mix_out(z, kv, yg, xf, mix_params, l, batch, seq, SEQ_ROWS)
        xf = _ffn(xf, ffn_params, l, seq, SEQ_ROWS)
    return xf.reshape(batch, seq, d)
```

```python
import functools
import math

import jax
import jax.numpy as jnp
from jax import lax
from jax.experimental import pallas as pl
from jax.experimental.pallas import tpu as pltpu

F32 = jnp.float32
BF16 = jnp.bfloat16

D_MODEL = 1024
MEM_LEN = 256
POOL_WIDTH = 512
POOL_GROUP_DIM = 128
POOL_WINDOWS = (2, 4, 8, 16)
POOL_HALO = 16
MEM_HEADS = 4
MEM_HEAD_DIM = 128
MEM_WIDTH = MEM_HEADS * MEM_HEAD_DIM
RWKV_HEAD_DIM = 64
RWKV_WIDTH = D_MODEL
DECAY_LORA = 64
ICLR_LORA = 64
VRES_LORA = 32
GATE_LORA = 128
LORA_COLS = DECAY_LORA + ICLR_LORA + GATE_LORA
RWKV_COLS = 3 * RWKV_WIDTH + LORA_COLS
N_BRANCH = 3
OFF_Q = POOL_WIDTH
OFF_RWKV = OFF_Q + MEM_WIDTH
OFF_GATE = OFF_RWKV + RWKV_COLS
IN_COLS = OFF_GATE + N_BRANCH * D_MODEL
D_FF = 2816
CONV_WIDTH = 3
NORM_EPS = 1e-6
LNX_EPS = 64e-5
L2_EPS = 1e-12

LANES = 128
SUBLANES = 8
VMEM_LIMIT_BYTES = 56 * 1024 * 1024

PROJ_ROWS = 1024
SEQ_ROWS = 512
SCAN_CHUNK = 64
SCAN_SEQS = 8
HEADS_PER_SLAB = LANES // RWKV_HEAD_DIM
N_SLABS = RWKV_WIDTH // LANES
FFN_COLS = 256
FFN_BUFS = 4
FFN_DOWN_SPLITS = ((0, 4), (4, 8), (8, 11))

ZBLK_PQ, ZBLK_G0, ZBLK_G1, ZBLK_G2 = range(4)


def _params(*sem):
    return pltpu.CompilerParams(dimension_semantics=sem, vmem_limit_bytes=VMEM_LIMIT_BYTES)


def _layer(a, l):
    return pl.BlockSpec((None,) + tuple(a.shape[1:]), lambda *_: (l,) + (0,) * (a.ndim - 1))


def _dot(a, b):
    return jnp.dot(a, b, preferred_element_type=F32)


def _dot_nt(a, b):
    return lax.dot_general(a, b, (((1,), (1,)), ((), ())), preferred_element_type=F32)


def _dot_tn(a, b):
    return lax.dot_general(a, b, (((0,), (0,)), ((), ())), preferred_element_type=F32)


def _sigmoid(x):
    return 1.0 / (1.0 + jnp.exp(-x))


def _rms(x, g):
    ms = jnp.mean(x * x, axis=-1, keepdims=True)
    return x * lax.rsqrt(ms + NORM_EPS) * g


def _split2(x):
    hi = x.astype(BF16)
    lo = (x - hi.astype(F32)).astype(BF16)
    return hi, lo


def _head_sums(xs, seg, two_term=False):
    rows = xs[0].shape[0]
    if not two_term:
        res = _dot(jnp.concatenate([x.astype(BF16) for x in xs], axis=0), seg)
        return [res[i * rows:(i + 1) * rows] for i in range(len(xs))]
    parts = [p for x in xs for p in _split2(x)]
    res = _dot(jnp.concatenate(parts, axis=0), seg)
    return [res[2 * i * rows:(2 * i + 1) * rows] + res[(2 * i + 1) * rows:(2 * i + 2) * rows]
            for i in range(len(xs))]


def _seg_matrix():
    i = lax.broadcasted_iota(jnp.int32, (LANES, LANES), 0) // RWKV_HEAD_DIM
    j = lax.broadcasted_iota(jnp.int32, (LANES, LANES), 1) // RWKV_HEAD_DIM
    return jnp.where(i == j, 1.0, 0.0).astype(BF16)


def _norm_matmul_kernel(x_ref, g_ref, w_ref, o_ref):
    o_ref[...] = _dot(_rms(x_ref[...], g_ref[...]).astype(BF16), w_ref[...]).astype(o_ref.dtype)


def _norm_matmul(x, g, w, l, tm, tn):
    t, d = x.shape
    n = w.shape[2]
    return pl.pallas_call(
        _norm_matmul_kernel,
        grid=(n // tn, t // tm),
        in_specs=[pl.BlockSpec((tm, d), lambda j, i: (i, 0)),
                  _layer(g, l),
                  pl.BlockSpec((None, d, tn), lambda j, i: (l, 0, j))],
        out_specs=pl.BlockSpec((tm, tn), lambda j, i: (i, j)),
        out_shape=jax.ShapeDtypeStruct((t, n), BF16),
        compiler_params=_params("parallel", "parallel"),
        name="norm_matmul",
    )(x, g, w)


def _mix_out_kernel(zpq_ref, halo_ref, zg0_ref, zg1_ref, zg2_ref, kv_ref, yg_ref, x_ref,
                    poolw_ref, poolb_ref, pools_ref, wpp_ref, wpm_ref, wpr_ref, wo_ref, gb_ref,
                    gn_ref, o_ref, *, tm):
    i = pl.program_id(1)
    hd = [slice(h * MEM_HEAD_DIM, (h + 1) * MEM_HEAD_DIM) for h in range(MEM_HEADS)]
    scores = [_dot_nt(zpq_ref[:, POOL_WIDTH + sl.start:POOL_WIDTH + sl.stop].astype(BF16),
                      kv_ref[:, sl].astype(BF16)) * (MEM_HEAD_DIM ** -0.5) for sl in hd]
    y_rwkv = _dot(yg_ref[...], wpr_ref[...])

    p = zpq_ref[:, :POOL_WIDTH].astype(F32)
    prev = jnp.where(i == 0, 0.0, halo_ref[...].astype(F32))
    pe = jnp.concatenate([prev, p], axis=0)
    pos = lax.broadcasted_iota(jnp.int32, (tm, POOL_GROUP_DIM), 0) + i * tm
    mixed = []
    for gi, win in enumerate(POOL_WINDOWS):
        sl = slice(gi * POOL_GROUP_DIM, (gi + 1) * POOL_GROUP_DIM)
        s = pe[:, sl]
        k = 1
        while k < win:
            s = s + pltpu.roll(s, k, axis=0)
            k *= 2
        cnt = jnp.minimum(pos + 1, win).astype(F32)
        pooled = s[POOL_HALO:] / cnt - p[:, sl]
        m = _dot(pooled.astype(BF16), poolw_ref[gi]) + poolb_ref[:, sl]
        mixed.append((m * pools_ref[:, sl]).astype(BF16))
    y_pool = _dot(jnp.concatenate(mixed, axis=1), wpp_ref[...])

    probs = []
    for s in scores:
        e = jnp.exp(s - jnp.max(s, axis=-1, keepdims=True))
        probs.append((e / jnp.sum(e, axis=-1, keepdims=True)).astype(BF16))
    heads = [_dot(probs[h], kv_ref[:, MEM_WIDTH + sl.start:MEM_WIDTH + sl.stop].astype(BF16)).astype(BF16)
             for h, sl in enumerate(hd)]
    y_mem = _dot(jnp.concatenate(heads, axis=1), wpm_ref[...])

    gate = lambda z_ref, n: _sigmoid(z_ref[...].astype(F32) + gb_ref[n:n + 1, :])
    merged = gate(zg0_ref, 0) * y_pool + gate(zg1_ref, 1) * y_rwkv + gate(zg2_ref, 2) * y_mem
    o = _dot(merged.astype(BF16), wo_ref[...])
    o_ref[...] = x_ref[...] + _rms(o, gn_ref[...])


def _mix_out(z, kv, yg, x, params, l, batch, seq, tm):
    t, d = x.shape
    nt = seq // tm
    hb = tm // POOL_HALO
    zblk = lambda blk: pl.BlockSpec((tm, d), lambda b, i: (b * nt + i, blk))
    return pl.pallas_call(
        functools.partial(_mix_out_kernel, tm=tm),
        grid=(batch, nt),
        in_specs=[zblk(ZBLK_PQ),
                  pl.BlockSpec((POOL_HALO, POOL_WIDTH),
                               lambda b, i: (jnp.maximum((b * nt + i) * hb - 1, 0), 0)),
                  zblk(ZBLK_G0), zblk(ZBLK_G1), zblk(ZBLK_G2),
                  pl.BlockSpec((MEM_LEN, 2 * MEM_WIDTH), lambda b, i: (b, 0)),
                  zblk(0), zblk(0)] + [_layer(a, l) for a in params],
        out_specs=zblk(0),
        out_shape=jax.ShapeDtypeStruct((t, d), F32),
        compiler_params=_params("parallel", "arbitrary"),
        name="mix_out",
    )(z, z, z, z, z, kv, yg, x, *params)


def _rwkv_in_kernel(*refs, tm, tiles_per_seq, has_vres):
    (x_ref, gn_ref, w_ref, mu_ref, w0_ref, wdec_ref, a0_ref, wa_ref, wg_ref, kk_ref, ka_ref) = refs[:11]
    refs = refs[11:]
    if has_vres:
        vfirst_ref, v0_ref, wdv_ref, wuv_ref = refs[:4]
        refs = refs[4:]
    r_o, lw_o, k_o, v_o, kk_o, b_o, g_o, carry_ref = refs
    d = RWKV_WIDTH

    xn = _rms(x_ref[...], gn_ref[...]).astype(BF16)
    first = (pl.program_id(0) % tiles_per_seq) == 0

    def project(lo, hi):
        z = _dot(xn, w_ref[:, lo:hi])
        last = jnp.where(first, 0.0, carry_ref[SUBLANES - 1:SUBLANES, lo:hi])
        carry_ref[:, lo:hi] = z[tm - SUBLANES:, :]
        row = lax.broadcasted_iota(jnp.int32, z.shape, 0)
        zp = jnp.where(row == 0, last, pltpu.roll(z, 1, axis=0))
        return z + (zp - z) * mu_ref[:, lo:hi]

    lora = project(3 * d, RWKV_COLS)
    dwa = lora[:, :LANES]
    dg = lora[:, LANES:]
    r_o[...] = project(0, d).astype(r_o.dtype)
    k = project(d, 2 * d)

    u = w0_ref[...] + _dot(jnp.tanh(dwa).astype(BF16), wdec_ref[...])
    lw_o[...] = -math.exp(-0.5) * _sigmoid(u)
    alpha = _sigmoid(a0_ref[...] + _dot(dwa.astype(BF16), wa_ref[...]))
    g_o[...] = _dot(_sigmoid(dg).astype(BF16), wg_ref[...]).astype(g_o.dtype)

    v = project(2 * d, 3 * d)

    seg = _seg_matrix()
    kk = k * kk_ref[...]
    kk2 = kk * kk
    ss = jnp.concatenate(_head_sums([kk2[:, s * LANES:(s + 1) * LANES] for s in range(N_SLABS)], seg), axis=1)
    kk = kk * lax.rsqrt(ss + L2_EPS)
    k_o[...] = (k * (1.0 + (alpha - 1.0) * ka_ref[...])).astype(k_o.dtype)
    kk_o[...] = kk.astype(kk_o.dtype)
    b_o[...] = (kk * alpha).astype(b_o.dtype)

    if has_vres:
        low = _dot(v.astype(BF16), wdv_ref[...])
        vg = _sigmoid(v0_ref[...] + _dot(low.astype(BF16), wuv_ref[...]))
        v = v + (vfirst_ref[...].astype(F32) - v) * vg
    v_o[...] = v.astype(v_o.dtype)


def _rwkv_in(x, params, l, v_first, vres_params, seq, tm):
    t, d = x.shape
    blk = pl.BlockSpec((tm, d), lambda i: (i, 0))
    in_specs = [blk] + [_layer(a, l) for a in params]
    args = [x, *params]
    if v_first is not None:
        in_specs += [blk] + [_layer(a, l - 1) for a in vres_params]
        args += [v_first, *vres_params]
    return pl.pallas_call(
        functools.partial(_rwkv_in_kernel, tm=tm, tiles_per_seq=seq // tm, has_vres=v_first is not None),
        grid=(t // tm,),
        in_specs=in_specs,
        out_specs=[blk] * 7,
        out_shape=[jax.ShapeDtypeStruct((t, d), F32 if n == 1 else BF16) for n in range(7)],
        scratch_shapes=[pltpu.VMEM((SUBLANES, RWKV_COLS), F32)],
        compiler_params=_params("arbitrary"),
        name="rwkv_in",
    )(*args)


def _stack_heads(x, m1):
    return jnp.concatenate([jnp.where(m1, x, 0.0), jnp.where(m1, 0.0, x)], axis=0)


def _scan_kernel(r_ref, lw_ref, k_ref, v_ref, kk_ref, b_ref, g_ref, rk_ref, lnw_ref, lnb_ref,
                 o_ref, h_ref):
    c = SCAN_CHUNK
    c2 = HEADS_PER_SLAB * c

    @pl.when(pl.program_id(1) == 0)
    def _():
        h_ref[...] = jnp.zeros_like(h_ref)

    lane = lax.broadcasted_iota(jnp.int32, (c, LANES), 1)
    m1 = lane < RWKV_HEAD_DIM
    ti = lax.broadcasted_iota(jnp.int32, (c, LANES), 0)
    tj = lane % RWKV_HEAD_DIM
    strict = ti > tj
    incl = ti >= tj
    eye = jnp.where(ti == tj, 1.0, 0.0)
    seg = _seg_matrix()
    inv_n = 1.0 / RWKV_HEAD_DIM
    ltri = jnp.where(lax.broadcasted_iota(jnp.int32, (c, c), 0)
                     >= lax.broadcasted_iota(jnp.int32, (c, c), 1), 1.0, 0.0).astype(BF16)

    units = [(j, slice(s * LANES, (s + 1) * LANES)) for j in range(SCAN_SEQS) for s in range(N_SLABS)]
    nu = range(len(units))

    ltri3 = jnp.concatenate([ltri] * 3, axis=1)
    lws, cums = [], []
    for j in range(SCAN_SEQS):
        lw = lw_ref[j]
        hi = lw.astype(BF16)
        r1 = lw - hi.astype(F32)
        mid = r1.astype(BF16)
        lo = (r1 - mid.astype(F32)).astype(BF16)
        lws.append(lw)
        cums.append(_dot(ltri3, jnp.concatenate([hi, mid, lo], axis=0)))

    ar, d_ak, d_r, tinv, apow, bk, decay_c = [], [], [], [], [], [], []
    for j, sl in units:
        cum_s = cums[j][:, sl]
        last = cum_s[c - 1:c, :]
        p_inv = jnp.exp(-cum_s)
        p_end = jnp.exp(last - cum_s)
        kx = k_ref[j, :, sl].astype(F32)
        bb = b_ref[j, :, sl].astype(F32)
        a_t = -kk_ref[j, :, sl].astype(F32) * jnp.exp(cum_s - lws[j][:, sl])
        r_t = r_ref[j, :, sl].astype(F32) * jnp.exp(cum_s)
        ar_s = jnp.concatenate([a_t, r_t], axis=0).astype(BF16)
        xr = jnp.concatenate([_stack_heads((bb * p_inv).astype(BF16), m1),
                              _stack_heads((kx * p_inv).astype(BF16), m1)], axis=0)
        gm = _dot_nt(ar_s, xr)
        d_ab = jnp.where(strict, gm[:c, :c2], 0.0)
        d_ak.append(jnp.where(strict, gm[:c, c2:], 0.0).astype(BF16))
        d_r.append(jnp.concatenate([jnp.where(incl, gm[c:, :c2], 0.0),
                                    jnp.where(incl, gm[c:, c2:], 0.0)], axis=1).astype(BF16))
        tinv.append(eye + d_ab)
        apow.append(d_ab.astype(BF16))
        ar.append(ar_s)
        bk.append(jnp.concatenate([_stack_heads((bb * p_end).astype(BF16), m1),
                                   _stack_heads((kx * p_end).astype(BF16), m1)], axis=0))
        decay_c.append(jnp.exp(last))

    for i in nu:
        apow[i] = _dot(apow[i], _stack_heads(apow[i], m1)).astype(BF16)
    n = 4
    while n < c:
        for i in nu:
            both = _dot(jnp.concatenate([tinv[i].astype(BF16), apow[i]], axis=0),
                        _stack_heads(apow[i], m1))
            tinv[i] = tinv[i] + both[:c]
            apow[i] = both[c:].astype(BF16)
        n *= 2
    for i in nu:
        tinv[i] = (tinv[i] + _dot(tinv[i].astype(BF16), _stack_heads(apow[i], m1))).astype(BF16)

    hs = [h_ref[i] for i in nu]
    xh = [_dot_nt(ar[i], hs[i].astype(BF16)) for i in nu]
    v_st = [_stack_heads(v_ref[j, :, sl], m1).astype(BF16) for j, sl in units]
    z = [xh[i][:c] + _dot(d_ak[i], v_st[i]) for i in nu]
    u = [_dot(tinv[i], _stack_heads(z[i].astype(BF16), m1)) for i in nu]
    uv = [jnp.concatenate([_stack_heads(u[i].astype(BF16), m1), v_st[i]], axis=0) for i in nu]
    y = [xh[i][c:] + _dot(d_r[i], uv[i]) for i in nu]
    for i in nu:
        h_ref[i] = hs[i] * decay_c[i] + _dot_tn(uv[i], bk[i])

    bonus = _head_sums([r_ref[j, :, sl].astype(F32) * k_ref[j, :, sl].astype(F32) * rk_ref[:, sl]
                        for j, sl in units], seg)
    mu = _head_sums(y, seg, two_term=True)
    dlt = [y[i] - mu[i] * inv_n for i in nu]
    var = _head_sums([dlt[i] * dlt[i] for i in nu], seg)
    for i, (j, sl) in enumerate(units):
        yn = dlt[i] * lax.rsqrt(var[i] * inv_n + LNX_EPS) * lnw_ref[:, sl] + lnb_ref[:, sl]
        o_ref[j, :, sl] = ((yn + bonus[i] * v_ref[j, :, sl].astype(F32))
                           * g_ref[j, :, sl].astype(F32)).astype(BF16)


def _scan(r, lw, k, v, kk, b, g, params, l, batch, seq):
    t, d = r.shape
    c = SCAN_CHUNK
    as_seqs = lambda a: a.reshape(batch, seq, d)
    blk = pl.BlockSpec((SCAN_SEQS, c, d), lambda bi, ci: (bi, ci, 0))
    out = pl.pallas_call(
        _scan_kernel,
        grid=(batch // SCAN_SEQS, seq // c),
        in_specs=[blk] * 7 + [_layer(a, l) for a in params],
        out_specs=blk,
        out_shape=jax.ShapeDtypeStruct((batch, seq, d), BF16),
        scratch_shapes=[pltpu.VMEM((SCAN_SEQS * N_SLABS, LANES, LANES), F32)],
        compiler_params=_params("parallel", "arbitrary"),
        name="rwkv_scan",
    )(*(as_seqs(a) for a in (r, lw, k, v, kk, b, g)), *params)
    return out.reshape(t, d)


GELU_C0 = math.sqrt(2.0 / math.pi)
GELU_C1 = GELU_C0 * 0.044715


def _gelu_tanh(x):
    hx = 0.5 * x
    return hx + hx * jnp.tanh(x * (GELU_C0 + GELU_C1 * (x * x)))


def _ffn_kernel(x_ref, gpre_ref, wup_ref, cw_ref, cb_ref, wd_ref, gpost_ref, o_ref,
                ubuf_ref, f_ref, carry_ref, *, tm, tiles_per_seq):
    first = (pl.program_id(0) % tiles_per_seq) == 0
    fc = FFN_COLS
    nf = D_FF // fc
    hn = _rms(x_ref[...], gpre_ref[...]).astype(BF16)
    for j in range(nf):
        halves = []
        for half in range(2):
            cols = slice(half * D_FF + j * fc, half * D_FF + (j + 1) * fc)
            buf = ubuf_ref.at[2 * (j % FFN_BUFS) + half]
            u = _dot(hn, wup_ref[:, cols])
            buf[0:SUBLANES, :] = jnp.where(first, 0.0, carry_ref[:, cols])
            buf[SUBLANES:, :] = u
            carry_ref[:, cols] = u[tm - SUBLANES:, :]
            out = cb_ref[:, cols] + cw_ref[0:1, cols] * buf[SUBLANES - 2:SUBLANES - 2 + tm, :]
            out = out + cw_ref[1:2, cols] * buf[SUBLANES - 1:SUBLANES - 1 + tm, :]
            halves.append(out + cw_ref[2:3, cols] * u)
        f_ref[:, j * fc:(j + 1) * fc] = (_gelu_tanh(halves[0]) * halves[1]).astype(BF16)
    acc = None
    for lo, hi in FFN_DOWN_SPLITS:
        part = _dot(f_ref[:, lo * fc:hi * fc], wd_ref[lo * fc:hi * fc, :])
        acc = part if acc is None else part + acc
    o_ref[...] = x_ref[...] + _rms(acc, gpost_ref[...])


def _ffn(x, params, l, seq, tm):
    t, d = x.shape
    xblk = pl.BlockSpec((tm, d), lambda i: (i, 0))
    return pl.pallas_call(
        functools.partial(_ffn_kernel, tm=tm, tiles_per_seq=seq // tm),
        grid=(t // tm,),
        in_specs=[xblk] + [_layer(a, l) for a in params],
        out_specs=xblk,
        out_shape=jax.ShapeDtypeStruct((t, d), F32),
        scratch_shapes=[pltpu.VMEM((2 * FFN_BUFS, SUBLANES + tm, FFN_COLS), F32),
                        pltpu.VMEM((tm, D_FF), BF16),
                        pltpu.VMEM((SUBLANES, 2 * D_FF), F32)],
        compiler_params=_params("arbitrary"),
        name="conv_ffn",
    )(x, *params)


def kernel(x, mem, mem_norm, norm_mix_pre, norm_mix_post, w_in, mu_shift, pool_w, pool_b,
           pool_scale, w_proj_pool, w_mem_kv, w_proj_mem, w0, w_up_decay, a0, w_up_a, w_up_g,
           k_k, k_a, r_k, ln_x_w, ln_x_b, v0, w_down_v, w_up_v, w_proj_rwkv, gate_b, w_o,
           norm_ffn_pre, norm_ffn_post, w_ffn_up, conv_w, conv_b, w_ffn_down):
    batch, seq, d = x.shape
    depth = w_in.shape[0]
    t = batch * seq
    assert d == D_MODEL and mem.shape[1] == MEM_LEN and batch % SCAN_SEQS == 0
    assert seq % SEQ_ROWS == 0 and t % PROJ_ROWS == 0 and (batch * MEM_LEN) % SEQ_ROWS == 0
    xf = x.reshape(t, d)
    memf = mem.reshape(batch * MEM_LEN, d)
    vec = lambda a: a.reshape(a.shape[0], 1, math.prod(a.shape[1:]))
    bf = lambda a: a.astype(BF16)
    pad = lambda a, rows, cols: jnp.pad(a, ((0, 0), rows, cols))
    w_pqg = bf(jnp.concatenate([w_in[:, :, :OFF_RWKV], w_in[:, :, OFF_GATE:]], axis=2))
    g_mix_pre = vec(norm_mix_pre)
    g_mem = jnp.broadcast_to(mem_norm.reshape(1, 1, d), (depth, 1, d))
    w_kv = bf(w_mem_kv)
    rwkv_in_params = [g_mix_pre, bf(w_in[:, :, OFF_RWKV:OFF_GATE]), vec(mu_shift), vec(w0),
                      bf(pad(w_up_decay, (0, LANES - DECAY_LORA), (0, 0))), vec(a0),
                      bf(pad(w_up_a, (DECAY_LORA, 0), (0, 0))), bf(w_up_g), vec(k_k), vec(k_a)]
    vres_params = [vec(v0), bf(pad(w_down_v, (0, 0), (0, LANES - VRES_LORA))),
                   bf(pad(w_up_v, (0, LANES - VRES_LORA), (0, 0)))]
    scan_params = [vec(r_k), vec(ln_x_w), vec(ln_x_b)]
    mix_params = [bf(pool_w), vec(pool_b), vec(pool_scale), bf(w_proj_pool), bf(w_proj_mem),
                  bf(w_proj_rwkv), bf(w_o), gate_b, vec(norm_mix_post)]
    ffn_params = [vec(norm_ffn_pre), bf(w_ffn_up), conv_w, vec(conv_b), bf(w_ffn_down), vec(norm_ffn_post)]

    v_first = None
    for l in range(depth):
        z = _norm_matmul(xf, g_mix_pre, w_pqg, l, PROJ_ROWS, w_pqg.shape[2] // 2)
        kv = _norm_matmul(memf, g_mem, w_kv, l, SEQ_ROWS, 2 * MEM_WIDTH)
        r, lw, k, v, kk, b, g = _rwkv_in(xf, rwkv_in_params, l, v_first, vres_params, seq, SEQ_ROWS)
        if l == 0:
            v_first = v
        yg = _scan(r, lw, k, v, kk, b, g, scan_params, l, batch, seq)
        xf = _mix_out(z, kv, yg, xf, mix_params, l, batch, seq, SEQ_ROWS)
        xf = _ffn(xf, ffn_params, l, seq, SEQ_ROWS)
    return xf.reshape(batch, seq, d)
```

```python
import functools
import math

import jax
import jax.numpy as jnp
from jax import lax
from jax.experimental import pallas as pl
from jax.experimental.pallas import tpu as pltpu

F32 = jnp.float32
BF16 = jnp.bfloat16

D_MODEL = 1024
MEM_LEN = 256
POOL_WIDTH = 512
POOL_GROUP_DIM = 128
POOL_WINDOWS = (2, 4, 8, 16)
POOL_HALO = 16
MEM_HEADS = 4
MEM_HEAD_DIM = 128
MEM_WIDTH = MEM_HEADS * MEM_HEAD_DIM
RWKV_HEAD_DIM = 64
RWKV_WIDTH = D_MODEL
DECAY_LORA = 64
ICLR_LORA = 64
VRES_LORA = 32
GATE_LORA = 128
LORA_COLS = DECAY_LORA + ICLR_LORA + GATE_LORA
RWKV_COLS = 3 * RWKV_WIDTH + LORA_COLS
N_BRANCH = 3
OFF_Q = POOL_WIDTH
OFF_RWKV = OFF_Q + MEM_WIDTH
OFF_GATE = OFF_RWKV + RWKV_COLS
IN_COLS = OFF_GATE + N_BRANCH * D_MODEL
D_FF = 2816
CONV_WIDTH = 3
NORM_EPS = 1e-6
LNX_EPS = 64e-5
L2_EPS = 1e-12

LANES = 128
SUBLANES = 8
VMEM_LIMIT_BYTES = 56 * 1024 * 1024

PROJ_ROWS = 1024
SEQ_ROWS = 512
SCAN_CHUNK = 64
SCAN_SEQS = 8
HEADS_PER_SLAB = LANES // RWKV_HEAD_DIM
N_SLABS = RWKV_WIDTH // LANES
FFN_COLS = 256
FFN_BUFS = 4
FFN_DOWN_SPLITS = ((0, 4), (4, 8), (8, 11))

ZBLK_PQ, ZBLK_G0, ZBLK_G1, ZBLK_G2 = range(4)


def _params(*sem):
    return pltpu.CompilerParams(dimension_semantics=sem, vmem_limit_bytes=VMEM_LIMIT_BYTES)


def _layer(a, l):
    return pl.BlockSpec((None,) + tuple(a.shape[1:]), lambda *_: (l,) + (0,) * (a.ndim - 1),
                        pipeline_mode=pl.Buffered(1))


def _dot(a, b):
    return jnp.dot(a, b, preferred_element_type=F32)


def _dot_nt(a, b):
    return lax.dot_general(a, b, (((1,), (1,)), ((), ())), preferred_element_type=F32)


def _dot_tn(a, b):
    return lax.dot_general(a, b, (((0,), (0,)), ((), ())), preferred_element_type=F32)


def _sigmoid(x):
    return 1.0 / (1.0 + jnp.exp(-x))


def _rms(x, g):
    ms = jnp.mean(x * x, axis=-1, keepdims=True)
    return x * lax.rsqrt(ms + NORM_EPS) * g


def _split2(x):
    hi = x.astype(BF16)
    lo = (x - hi.astype(F32)).astype(BF16)
    return hi, lo


def _head_sums(xs, seg, two_term=False):
    rows = xs[0].shape[0]
    if not two_term:
        res = _dot(jnp.concatenate([x.astype(BF16) for x in xs], axis=0), seg)
        return [res[i * rows:(i + 1) * rows] for i in range(len(xs))]
    parts = [p for x in xs for p in _split2(x)]
    res = _dot(jnp.concatenate(parts, axis=0), seg)
    return [res[2 * i * rows:(2 * i + 1) * rows] + res[(2 * i + 1) * rows:(2 * i + 2) * rows]
            for i in range(len(xs))]


def _seg_matrix():
    i = lax.broadcasted_iota(jnp.int32, (LANES, LANES), 0) // RWKV_HEAD_DIM
    j = lax.broadcasted_iota(jnp.int32, (LANES, LANES), 1) // RWKV_HEAD_DIM
    return jnp.where(i == j, 1.0, 0.0).astype(BF16)


def _norm_matmul_kernel(x_ref, g_ref, w_ref, o_ref):
    o_ref[...] = _dot(_rms(x_ref[...], g_ref[...]).astype(BF16), w_ref[...]).astype(o_ref.dtype)


def _norm_matmul(x, g, w, l, tm, tn):
    t, d = x.shape
    n = w.shape[2]
    return pl.pallas_call(
        _norm_matmul_kernel,
        grid=(n // tn, t // tm),
        in_specs=[pl.BlockSpec((tm, d), lambda j, i: (i, 0)),
                  _layer(g, l),
                  pl.BlockSpec((None, d, tn), lambda j, i: (l, 0, j))],
        out_specs=pl.BlockSpec((tm, tn), lambda j, i: (i, j)),
        out_shape=jax.ShapeDtypeStruct((t, n), BF16),
        compiler_params=_params("parallel", "parallel"),
        name="norm_matmul",
    )(x, g, w)


def _mix_out_kernel(zpq_ref, halo_ref, zg0_ref, zg1_ref, zg2_ref, kv_ref, yg_ref, x_ref,
                    poolw_ref, poolb_ref, pools_ref, wpp_ref, wpm_ref, wpr_ref, wo_ref, gb_ref,
                    gn_ref, o_ref, *, tm):
    i = pl.program_id(1)
    hd = [slice(h * MEM_HEAD_DIM, (h + 1) * MEM_HEAD_DIM) for h in range(MEM_HEADS)]
    scores = [_dot_nt(zpq_ref[:, POOL_WIDTH + sl.start:POOL_WIDTH + sl.stop].astype(BF16),
                      kv_ref[:, sl].astype(BF16)) * (MEM_HEAD_DIM ** -0.5) for sl in hd]
    y_rwkv = _dot(yg_ref[...], wpr_ref[...])

    p = zpq_ref[:, :POOL_WIDTH].astype(F32)
    prev = jnp.where(i == 0, 0.0, halo_ref[...].astype(F32))
    pe = jnp.concatenate([prev, p], axis=0)
    pos = lax.broadcasted_iota(jnp.int32, (tm, POOL_GROUP_DIM), 0) + i * tm
    mixed = []
    for gi, win in enumerate(POOL_WINDOWS):
        sl = slice(gi * POOL_GROUP_DIM, (gi + 1) * POOL_GROUP_DIM)
        s = pe[:, sl]
        k = 1
        while k < win:
            s = s + pltpu.roll(s, k, axis=0)
            k *= 2
        cnt = jnp.minimum(pos + 1, win).astype(F32)
        pooled = s[POOL_HALO:] / cnt - p[:, sl]
        m = _dot(pooled.astype(BF16), poolw_ref[gi]) + poolb_ref[:, sl]
        mixed.append((m * pools_ref[:, sl]).astype(BF16))
    y_pool = _dot(jnp.concatenate(mixed, axis=1), wpp_ref[...])

    probs = []
    for s in scores:
        e = jnp.exp(s - jnp.max(s, axis=-1, keepdims=True))
        probs.append((e / jnp.sum(e, axis=-1, keepdims=True)).astype(BF16))
    heads = [_dot(probs[h], kv_ref[:, MEM_WIDTH + sl.start:MEM_WIDTH + sl.stop].astype(BF16)).astype(BF16)
             for h, sl in enumerate(hd)]
    y_mem = _dot(jnp.concatenate(heads, axis=1), wpm_ref[...])

    gate = lambda z_ref, n: _sigmoid(z_ref[...].astype(F32) + gb_ref[n:n + 1, :])
    merged = gate(zg0_ref, 0) * y_pool + gate(zg1_ref, 1) * y_rwkv + gate(zg2_ref, 2) * y_mem
    o = _dot(merged.astype(BF16), wo_ref[...])
    o_ref[...] = x_ref[...] + _rms(o, gn_ref[...])


def _mix_out(z, kv, yg, x, params, l, batch, seq, tm):
    t, d = x.shape
    nt = seq // tm
    hb = tm // POOL_HALO
    zblk = lambda blk: pl.BlockSpec((tm, d), lambda b, i: (b * nt + i, blk))
    return pl.pallas_call(
        functools.partial(_mix_out_kernel, tm=tm),
        grid=(batch, nt),
        in_specs=[zblk(ZBLK_PQ),
                  pl.BlockSpec((POOL_HALO, POOL_WIDTH),
                               lambda b, i: (jnp.maximum((b * nt + i) * hb - 1, 0), 0)),
                  zblk(ZBLK_G0), zblk(ZBLK_G1), zblk(ZBLK_G2),
                  pl.BlockSpec((MEM_LEN, 2 * MEM_WIDTH), lambda b, i: (b, 0)),
                  zblk(0), zblk(0)] + [_layer(a, l) for a in params],
        out_specs=zblk(0),
        out_shape=jax.ShapeDtypeStruct((t, d), F32),
        compiler_params=_params("parallel", "arbitrary"),
        name="mix_out",
    )(z, z, z, z, z, kv, yg, x, *params)


def _rwkv_in_kernel(*refs, tm, tiles_per_seq, has_vres):
    (x_ref, gn_ref, w_ref, mu_ref, w0_ref, wdec_ref, a0_ref, wa_ref, wg_ref, kk_ref, ka_ref) = refs[:11]
    refs = refs[11:]
    if has_vres:
        vfirst_ref, v0_ref, wdv_ref, wuv_ref = refs[:4]
        refs = refs[4:]
    r_o, lw_o, k_o, v_o, kk_o, b_o, g_o, carry_ref = refs
    d = RWKV_WIDTH

    xn = _rms(x_ref[...], gn_ref[...]).astype(BF16)
    first = (pl.program_id(0) % tiles_per_seq) == 0

    def project(lo, hi):
        z = _dot(xn, w_ref[:, lo:hi])
        last = jnp.where(first, 0.0, carry_ref[SUBLANES - 1:SUBLANES, lo:hi])
        carry_ref[:, lo:hi] = z[tm - SUBLANES:, :]
        row = lax.broadcasted_iota(jnp.int32, z.shape, 0)
        zp = jnp.where(row == 0, last, pltpu.roll(z, 1, axis=0))
        return z + (zp - z) * mu_ref[:, lo:hi]

    lora = project(3 * d, RWKV_COLS)
    dwa = lora[:, :LANES]
    dg = lora[:, LANES:]
    r_o[...] = project(0, d).astype(r_o.dtype)
    k = project(d, 2 * d)

    u = w0_ref[...] + _dot(jnp.tanh(dwa).astype(BF16), wdec_ref[...])
    lw_o[...] = -math.exp(-0.5) * _sigmoid(u)
    alpha = _sigmoid(a0_ref[...] + _dot(dwa.astype(BF16), wa_ref[...]))
    g_o[...] = _dot(_sigmoid(dg).astype(BF16), wg_ref[...]).astype(g_o.dtype)

    v = project(2 * d, 3 * d)

    seg = _seg_matrix()
    kk = k * kk_ref[...]
    kk2 = kk * kk
    ss = jnp.concatenate(_head_sums([kk2[:, s * LANES:(s + 1) * LANES] for s in range(N_SLABS)], seg), axis=1)
    kk = kk * lax.rsqrt(ss + L2_EPS)
    k_o[...] = (k * (1.0 + (alpha - 1.0) * ka_ref[...])).astype(k_o.dtype)
    kk_o[...] = kk.astype(kk_o.dtype)
    b_o[...] = (kk * alpha).astype(b_o.dtype)

    if has_vres:
        low = _dot(v.astype(BF16), wdv_ref[...])
        vg = _sigmoid(v0_ref[...] + _dot(low.astype(BF16), wuv_ref[...]))
        v = v + (vfirst_ref[...].astype(F32) - v) * vg
    v_o[...] = v.astype(v_o.dtype)


def _rwkv_in(x, params, l, v_first, vres_params, seq, tm):
    t, d = x.shape
    blk = pl.BlockSpec((tm, d), lambda i: (i, 0))
    in_specs = [blk] + [_layer(a, l) for a in params]
    args = [x, *params]
    if v_first is not None:
        in_specs += [blk] + [_layer(a, l - 1) for a in vres_params]
        args += [v_first, *vres_params]
    return pl.pallas_call(
        functools.partial(_rwkv_in_kernel, tm=tm, tiles_per_seq=seq // tm, has_vres=v_first is not None),
        grid=(t // tm,),
        in_specs=in_specs,
        out_specs=[blk] * 7,
        out_shape=[jax.ShapeDtypeStruct((t, d), F32 if n == 1 else BF16) for n in range(7)],
        scratch_shapes=[pltpu.VMEM((SUBLANES, RWKV_COLS), F32)],
        compiler_params=_params("arbitrary"),
        name="rwkv_in",
    )(*args)


def _stack_heads(x, m1):
    return jnp.concatenate([jnp.where(m1, x, 0.0), jnp.where(m1, 0.0, x)], axis=0)


def _scan_kernel(r_ref, lw_ref, k_ref, v_ref, kk_ref, b_ref, g_ref, rk_ref, lnw_ref, lnb_ref,
                 o_ref, h_ref):
    c = SCAN_CHUNK
    c2 = HEADS_PER_SLAB * c

    @pl.when(pl.program_id(1) == 0)
    def _():
        h_ref[...] = jnp.zeros_like(h_ref)

    lane = lax.broadcasted_iota(jnp.int32, (c, LANES), 1)
    m1 = lane < RWKV_HEAD_DIM
    ti = lax.broadcasted_iota(jnp.int32, (c, LANES), 0)
    tj = lane % RWKV_HEAD_DIM
    strict = ti > tj
    incl = ti >= tj
    eye = jnp.where(ti == tj, 1.0, 0.0)
    seg = _seg_matrix()
    inv_n = 1.0 / RWKV_HEAD_DIM
    ltri = jnp.where(lax.broadcasted_iota(jnp.int32, (c, c), 0)
                     >= lax.broadcasted_iota(jnp.int32, (c, c), 1), 1.0, 0.0).astype(BF16)

    units = [(j, slice(s * LANES, (s + 1) * LANES)) for j in range(SCAN_SEQS) for s in range(N_SLABS)]
    nu = range(len(units))

    ltri3 = jnp.concatenate([ltri] * 3, axis=1)
    lws, cums = [], []
    for j in range(SCAN_SEQS):
        lw = lw_ref[j]
        hi = lw.astype(BF16)
        r1 = lw - hi.astype(F32)
        mid = r1.astype(BF16)
        lo = (r1 - mid.astype(F32)).astype(BF16)
        lws.append(lw)
        cums.append(_dot(ltri3, jnp.concatenate([hi, mid, lo], axis=0)))

    ar, d_ak, d_r, tinv, apow, bk, decay_c = [], [], [], [], [], [], []
    for j, sl in units:
        cum_s = cums[j][:, sl]
        last = cum_s[c - 1:c, :]
        p_inv = jnp.exp(-cum_s)
        p_end = jnp.exp(last - cum_s)
        kx = k_ref[j, :, sl].astype(F32)
        bb = b_ref[j, :, sl].astype(F32)
        a_t = -kk_ref[j, :, sl].astype(F32) * jnp.exp(cum_s - lws[j][:, sl])
        r_t = r_ref[j, :, sl].astype(F32) * jnp.exp(cum_s)
        ar_s = jnp.concatenate([a_t, r_t], axis=0).astype(BF16)
        xr = jnp.concatenate([_stack_heads((bb * p_inv).astype(BF16), m1),
                              _stack_heads((kx * p_inv).astype(BF16), m1)], axis=0)
        gm = _dot_nt(ar_s, xr)
        d_ab = jnp.where(strict, gm[:c, :c2], 0.0)
        d_ak.append(jnp.where(strict, gm[:c, c2:], 0.0).astype(BF16))
        d_r.append(jnp.concatenate([jnp.where(incl, gm[c:, :c2], 0.0),
                                    jnp.where(incl, gm[c:, c2:], 0.0)], axis=1).astype(BF16))
        tinv.append(eye + d_ab)
        apow.append(d_ab.astype(BF16))
        ar.append(ar_s)
        bk.append(jnp.concatenate([_stack_heads((bb * p_end).astype(BF16), m1),
                                   _stack_heads((kx * p_end).astype(BF16), m1)], axis=0))
        decay_c.append(jnp.exp(last))

    for i in nu:
        apow[i] = _dot(apow[i], _stack_heads(apow[i], m1)).astype(BF16)
    n = 4
    while n < c:
        for i in nu:
            both = _dot(jnp.concatenate([tinv[i].astype(BF16), apow[i]], axis=0),
                        _stack_heads(apow[i], m1))
            tinv[i] = tinv[i] + both[:c]
            apow[i] = both[c:].astype(BF16)
        n *= 2
    for i in nu:
        tinv[i] = (tinv[i] + _dot(tinv[i].astype(BF16), _stack_heads(apow[i], m1))).astype(BF16)

    hs = [h_ref[i] for i in nu]
    xh = [_dot_nt(ar[i], hs[i].astype(BF16)) for i in nu]
    v_st = [_stack_heads(v_ref[j, :, sl], m1).astype(BF16) for j, sl in units]
    z = [xh[i][:c] + _dot(d_ak[i], v_st[i]) for i in nu]
    u = [_dot(tinv[i], _stack_heads(z[i].astype(BF16), m1)) for i in nu]
    uv = [jnp.concatenate([_stack_heads(u[i].astype(BF16), m1), v_st[i]], axis=0) for i in nu]
    y = [xh[i][c:] + _dot(d_r[i], uv[i]) for i in nu]
    for i in nu:
        h_ref[i] = hs[i] * decay_c[i] + _dot_tn(uv[i], bk[i])

    bonus = _head_sums([r_ref[j, :, sl].astype(F32) * k_ref[j, :, sl].astype(F32) * rk_ref[:, sl]
                        for j, sl in units], seg)
    mu = _head_sums(y, seg, two_term=True)
    dlt = [y[i] - mu[i] * inv_n for i in nu]
    var = _head_sums([dlt[i] * dlt[i] for i in nu], seg)
    for i, (j, sl) in enumerate(units):
        yn = dlt[i] * lax.rsqrt(var[i] * inv_n + LNX_EPS) * lnw_ref[:, sl] + lnb_ref[:, sl]
        o_ref[j, :, sl] = ((yn + bonus[i] * v_ref[j, :, sl].astype(F32))
                           * g_ref[j, :, sl].astype(F32)).astype(BF16)


def _scan(r, lw, k, v, kk, b, g, params, l, batch, seq):
    t, d = r.shape
    c = SCAN_CHUNK
    as_seqs = lambda a: a.reshape(batch, seq, d)
    blk = pl.BlockSpec((SCAN_SEQS, c, d), lambda bi, ci: (bi, ci, 0))
    out = pl.pallas_call(
        _scan_kernel,
        grid=(batch // SCAN_SEQS, seq // c),
        in_specs=[blk] * 7 + [_layer(a, l) for a in params],
        out_specs=blk,
        out_shape=jax.ShapeDtypeStruct((batch, seq, d), BF16),
        scratch_shapes=[pltpu.VMEM((SCAN_SEQS * N_SLABS, LANES, LANES), F32)],
        compiler_params=_params("parallel", "arbitrary"),
        name="rwkv_scan",
    )(*(as_seqs(a) for a in (r, lw, k, v, kk, b, g)), *params)
    return out.reshape(t, d)


GELU_C0 = math.sqrt(2.0 / math.pi)
GELU_C1 = GELU_C0 * 0.044715


def _gelu_tanh(x):
    hx = 0.5 * x
    return hx + hx * jnp.tanh(x * (GELU_C0 + GELU_C1 * (x * x)))


def _ffn_kernel(x_ref, gpre_ref, wup_ref, cw_ref, cb_ref, wd_ref, gpost_ref, o_ref,
                ubuf_ref, f_ref, carry_ref, *, tm, tiles_per_seq):
    first = (pl.program_id(0) % tiles_per_seq) == 0
    fc = FFN_COLS
    nf = D_FF // fc
    hn = _rms(x_ref[...], gpre_ref[...]).astype(BF16)
    for j in range(nf):
        halves = []
        for half in range(2):
            cols = slice(half * D_FF + j * fc, half * D_FF + (j + 1) * fc)
            buf = ubuf_ref.at[2 * (j % FFN_BUFS) + half]
            u = _dot(hn, wup_ref[:, cols])
            buf[0:SUBLANES, :] = jnp.where(first, 0.0, carry_ref[:, cols])
            buf[SUBLANES:, :] = u
            carry_ref[:, cols] = u[tm - SUBLANES:, :]
            out = cb_ref[:, cols] + cw_ref[0:1, cols] * buf[SUBLANES - 2:SUBLANES - 2 + tm, :]
            out = out + cw_ref[1:2, cols] * buf[SUBLANES - 1:SUBLANES - 1 + tm, :]
            halves.append(out + cw_ref[2:3, cols] * u)
        f_ref[:, j * fc:(j + 1) * fc] = (_gelu_tanh(halves[0]) * halves[1]).astype(BF16)
    acc = None
    for lo, hi in FFN_DOWN_SPLITS:
        part = _dot(f_ref[:, lo * fc:hi * fc], wd_ref[lo * fc:hi * fc, :])
        acc = part if acc is None else part + acc
    o_ref[...] = x_ref[...] + _rms(acc, gpost_ref[...])


def _ffn(x, params, l, seq, tm):
    t, d = x.shape
    xblk = pl.BlockSpec((tm, d), lambda i: (i, 0))
    return pl.pallas_call(
        functools.partial(_ffn_kernel, tm=tm, tiles_per_seq=seq // tm),
        grid=(t // tm,),
        in_specs=[xblk] + [_layer(a, l) for a in params],
        out_specs=xblk,
        out_shape=jax.ShapeDtypeStruct((t, d), F32),
        scratch_shapes=[pltpu.VMEM((2 * FFN_BUFS, SUBLANES + tm, FFN_COLS), F32),
                        pltpu.VMEM((tm, D_FF), BF16),
                        pltpu.VMEM((SUBLANES, 2 * D_FF), F32)],
        compiler_params=_params("arbitrary"),
        name="conv_ffn",
    )(x, *params)


def kernel(x, mem, mem_norm, norm_mix_pre, norm_mix_post, w_in, mu_shift, pool_w, pool_b,
           pool_scale, w_proj_pool, w_mem_kv, w_proj_mem, w0, w_up_decay, a0, w_up_a, w_up_g,
           k_k, k_a, r_k, ln_x_w, ln_x_b, v0, w_down_v, w_up_v, w_proj_rwkv, gate_b, w_o,
           norm_ffn_pre, norm_ffn_post, w_ffn_up, conv_w, conv_b, w_ffn_down):
    batch, seq, d = x.shape
    depth = w_in.shape[0]
    t = batch * seq
    assert d == D_MODEL and mem.shape[1] == MEM_LEN and batch % SCAN_SEQS == 0
    assert seq % SEQ_ROWS == 0 and t % PROJ_ROWS == 0 and (batch * MEM_LEN) % SEQ_ROWS == 0
    xf = x.reshape(t, d)
    memf = mem.reshape(batch * MEM_LEN, d)
    vec = lambda a: a.reshape(a.shape[0], 1, math.prod(a.shape[1:]))
    bf = lambda a: a.astype(BF16)
    pad = lambda a, rows, cols: jnp.pad(a, ((0, 0), rows, cols))
    w_pqg = bf(jnp.concatenate([w_in[:, :, :OFF_RWKV], w_in[:, :, OFF_GATE:]], axis=2))
    g_mix_pre = vec(norm_mix_pre)
    g_mem = jnp.broadcast_to(mem_norm.reshape(1, 1, d), (depth, 1, d))
    w_kv = bf(w_mem_kv)
    rwkv_in_params = [g_mix_pre, bf(w_in[:, :, OFF_RWKV:OFF_GATE]), vec(mu_shift), vec(w0),
                      bf(pad(w_up_decay, (0, LANES - DECAY_LORA), (0, 0))), vec(a0),
                      bf(pad(w_up_a, (DECAY_LORA, 0), (0, 0))), bf(w_up_g), vec(k_k), vec(k_a)]
    vres_params = [vec(v0), bf(pad(w_down_v, (0, 0), (0, LANES - VRES_LORA))),
                   bf(pad(w_up_v, (0, LANES - VRES_LORA), (0, 0)))]
    scan_params = [vec(r_k), vec(ln_x_w), vec(ln_x_b)]
    mix_params = [bf(pool_w), vec(pool_b), vec(pool_scale), bf(w_proj_pool), bf(w_proj_mem),
                  bf(w_proj_rwkv), bf(w_o), gate_b, vec(norm_mix_post)]
    ffn_params = [vec(norm_ffn_pre), bf(w_ffn_up), conv_w, vec(conv_b), bf(w_ffn_down), vec(norm_ffn_post)]

    v_first = None
    for l in range(depth):
        z = _norm_matmul(xf, g_mix_pre, w_pqg, l, PROJ_ROWS, w_pqg.shape[2] // 2)
        kv = _norm_matmul(memf, g_mem, w_kv, l, SEQ_ROWS, 2 * MEM_WIDTH)
        r, lw, k, v, kk, b, g = _rwkv_in(xf, rwkv_in_params, l, v_first, vres_params, seq, SEQ_ROWS)
        if l == 0:
            v_first = v
        yg = _scan(r, lw, k, v, kk, b, g, scan_params, l, batch, seq)
        xf = _mix_out(z, kv, yg, xf, mix_params, l, batch, seq, SEQ_ROWS)
        xf = _ffn(xf, ffn_params, l, seq, SEQ_ROWS)
    return xf.reshape(batch, seq, d)
```
